```python
import jax, jax.numpy as jnp
from jax import lax
import numpy as np

D_MODEL = 1024
BATCH = 8
SEQ = 4096
DEPTH = 4

N_HEADS = 16
HEAD_DIM = D_MODEL // N_HEADS
D_FF = 2816
CONV_WIDTH = 3
Q_BLOCK = 128
NORM_EPS = 1e-6
ATTN_SCALE = HEAD_DIM ** -0.5
N_MIXERS = 2
N_FOX = (DEPTH + 1) // 2
N_SB = DEPTH // 2

kernel_name = "hybrid_fox_stickbreaking_convffn"


def rms_norm(x, g):
    xf = x.astype(jnp.float32)
    y = xf * lax.rsqrt(jnp.mean(xf * xf, axis=-1, keepdims=True) + NORM_EPS)
    return (y * g.astype(jnp.float32)).astype(x.dtype)


def split_heads(t):
    b, s, _ = t.shape
    return t.reshape(b, s, N_HEADS, HEAD_DIM).transpose(0, 2, 1, 3)


def merge_heads(t):
    b, h, s, dh = t.shape
    return t.transpose(0, 2, 1, 3).reshape(b, s, h * dh)


def sweep_query_blocks(block_fn, n_blocks, b, s):
    out = lax.map(block_fn, jnp.arange(n_blocks))
    return out.transpose(1, 2, 0, 3, 4).reshape(b, N_HEADS, s, HEAD_DIM)


def fox_mixer(xn, w_qkvf, b_f, w_o):
    b, s, _ = xn.shape
    proj = xn @ w_qkvf
    q = split_heads(proj[..., :D_MODEL]).astype(jnp.float32)
    k = split_heads(proj[..., D_MODEL:2 * D_MODEL]).astype(jnp.float32)
    v = split_heads(proj[..., 2 * D_MODEL:3 * D_MODEL]).astype(jnp.float32)
    f_logit = (proj[..., 3 * D_MODEL:] + b_f).astype(jnp.float32)
    c = jnp.cumsum(jax.nn.log_sigmoid(f_logit), axis=1).transpose(0, 2, 1)
    key_pos = jnp.arange(s)

    def block(i):
        start = i * Q_BLOCK
        qb = lax.dynamic_slice_in_dim(q, start, Q_BLOCK, axis=2)
        cq = lax.dynamic_slice_in_dim(c, start, Q_BLOCK, axis=2)
        q_pos = start + jnp.arange(Q_BLOCK)
        logits = (jnp.einsum('bhqd,bhkd->bhqk', qb, k) * ATTN_SCALE
                  + cq[..., :, None] - c[..., None, :])
        mask = key_pos[None, :] <= q_pos[:, None]
        logits = jnp.where(mask, logits, -jnp.inf)
        p = jax.nn.softmax(logits, axis=-1)
        return jnp.einsum('bhqk,bhkd->bhqd', p, v)

    o = sweep_query_blocks(block, s // Q_BLOCK, b, s)
    return merge_heads(o).astype(xn.dtype) @ w_o


def stick_breaking_mixer(xn, w_qkv, w_o):
    b, s, _ = xn.shape
    proj = xn @ w_qkv
    q = split_heads(proj[..., :D_MODEL]).astype(jnp.float32)
    k = split_heads(proj[..., D_MODEL:2 * D_MODEL]).astype(jnp.float32)
    v = split_heads(proj[..., 2 * D_MODEL:]).astype(jnp.float32)
    key_pos = jnp.arange(s)

    def block(i):
        start = i * Q_BLOCK
        qb = lax.dynamic_slice_in_dim(q, start, Q_BLOCK, axis=2)
        q_pos = start + jnp.arange(Q_BLOCK)
        z = jnp.einsum('bhqd,bhkd->bhqk', qb, k) * ATTN_SCALE
        mask = key_pos[None, :] < q_pos[:, None]
        log_not_beta = jnp.where(mask, jax.nn.log_sigmoid(-z), 0.0)
        tail = lax.cumsum(log_not_beta, axis=3, reverse=True) - log_not_beta
        log_a = jax.nn.log_sigmoid(z) + tail
        a = jnp.where(mask, jnp.exp(log_a), 0.0)
        return jnp.einsum('bhqk,bhkd->bhqd', a, v)

    o = sweep_query_blocks(block, s // Q_BLOCK, b, s)
    return merge_heads(o).astype(xn.dtype) @ w_o


def conv_ffn(xn, w_up, w_conv, b_conv, w_down):
    s = xn.shape[1]
    h = xn @ w_up
    h_pad = jnp.pad(h, ((0, 0), (CONV_WIDTH - 1, 0), (0, 0)))
    hc = b_conv + sum(w_conv[j] * h_pad[:, j:j + s, :] for j in range(CONV_WIDTH))
    u, g = hc[..., :D_FF], hc[..., D_FF:]
    return (jax.nn.silu(g) * u) @ w_down


def setup_inputs(seed: int = 0) -> dict:
    key = jax.random.key(seed)
    ks = jax.random.split(key, 16)
    f32 = jnp.float32
    d, h, f = D_MODEL, N_HEADS, D_FF
    x = jax.random.normal(ks[0], (BATCH, SEQ, d), f32)
    attn_norm = 1.0 + 0.02 * jax.random.normal(ks[1], (DEPTH, d), f32)
    ffn_norm = 1.0 + 0.02 * jax.random.normal(ks[2], (DEPTH, d), f32)
    final_norm = 1.0 + 0.02 * jax.random.normal(ks[3], (d,), f32)
    fox_w_qkvf = jax.random.normal(ks[4], (N_FOX, d, 3 * d + h), f32) * d ** -0.5
    fox_b_f = 3.0 + 0.5 * jax.random.normal(ks[5], (N_FOX, h), f32)
    fox_w_o = jax.random.normal(ks[6], (N_FOX, d, d), f32) * d ** -0.5
    sb_w_qkv = jax.random.normal(ks[7], (N_SB, d, 3 * d), f32) * d ** -0.5
    sb_w_o = jax.random.normal(ks[8], (N_SB, d, d), f32) * d ** -0.5
    ffn_w_up = jax.random.normal(ks[9], (DEPTH, d, 2 * f), f32) * d ** -0.5
    ffn_w_conv = jax.random.normal(ks[10], (DEPTH, CONV_WIDTH, 2 * f), f32) * CONV_WIDTH ** -0.5
    ffn_b_conv = 0.01 * jax.random.normal(ks[11], (DEPTH, 2 * f), f32)
    ffn_w_down = jax.random.normal(ks[12], (DEPTH, f, d), f32) * f ** -0.5
    return {"x": x, "attn_norm": attn_norm, "ffn_norm": ffn_norm, "final_norm": final_norm,
            "fox_w_qkvf": fox_w_qkvf, "fox_b_f": fox_b_f, "fox_w_o": fox_w_o,
            "sb_w_qkv": sb_w_qkv, "sb_w_o": sb_w_o,
            "ffn_w_up": ffn_w_up, "ffn_w_conv": ffn_w_conv, "ffn_b_conv": ffn_b_conv,
            "ffn_w_down": ffn_w_down}


def reference(x, attn_norm, ffn_norm, final_norm, fox_w_qkvf, fox_b_f, fox_w_o,
              sb_w_qkv, sb_w_o, ffn_w_up, ffn_w_conv, ffn_b_conv, ffn_w_down):
    hs = x
    for i in range(DEPTH):
        xn = rms_norm(hs, attn_norm[i])
        if i % N_MIXERS == 0:
            j = i // N_MIXERS
            hs = hs + fox_mixer(xn, fox_w_qkvf[j], fox_b_f[j], fox_w_o[j])
        else:
            j = i // N_MIXERS
            hs = hs + stick_breaking_mixer(xn, sb_w_qkv[j], sb_w_o[j])
        xn = rms_norm(hs, ffn_norm[i])
        hs = hs + conv_ffn(xn, ffn_w_up[i], ffn_w_conv[i], ffn_b_conv[i], ffn_w_down[i])
    return rms_norm(hs, final_norm)
```

```python
import functools
import math

import jax
import jax.numpy as jnp
from jax import lax
from jax.experimental import pallas as pl
from jax.experimental.pallas import tpu as pltpu

D_MODEL = 1024
N_HEADS = 16
HEAD_DIM = D_MODEL // N_HEADS
D_FF = 2816
CONV_WIDTH = 3
DEPTH = 4
NORM_EPS = 1e-6
ATTN_SCALE = HEAD_DIM ** -0.5
LOG2E = math.log2(math.e)

F32 = jnp.float32
BF16 = jnp.bfloat16

LANES = 128
HEADS_PER_STEP = LANES // HEAD_DIM
N_HEAD_PAIRS = N_HEADS // HEADS_PER_STEP
ROW_TILE = 512
PROJ_CHUNK = 512
FF_CHUNK = 256
N_FF_CHUNKS = D_FF // FF_CHUNK
TQ = 256
TK = 256
CUM_BLOCK = 128
MIB = 2 ** 20


def _params(semantics, vmem_mib):
    return pltpu.CompilerParams(dimension_semantics=semantics,
                                vmem_limit_bytes=vmem_mib * MIB)


def _dot(a, b):
    return jnp.dot(a, b, preferred_element_type=F32)


def _dot_nt(a, b):
    return lax.dot_general(a, b, (((1,), (1,)), ((), ())), preferred_element_type=F32)


def _rms(h, g):
    ms = jnp.mean(h * h, axis=-1, keepdims=True)
    return h * lax.rsqrt(ms + NORM_EPS) * g


def _rmsnorm_kernel(x_ref, g_ref, o_ref):
    o_ref[...] = _rms(x_ref[...], g_ref[...]).astype(o_ref.dtype)


def _rmsnorm(x, g, out_dtype):
    m, d = x.shape
    return pl.pallas_call(
        _rmsnorm_kernel,
        grid=(m // ROW_TILE,),
        in_specs=[pl.BlockSpec((ROW_TILE, d), lambda i: (i, 0)),
                  pl.BlockSpec((1, d), lambda i: (0, 0))],
        out_specs=pl.BlockSpec((ROW_TILE, d), lambda i: (i, 0)),
        out_shape=jax.ShapeDtypeStruct((m, d), out_dtype),
        compiler_params=_params(("parallel",), 32),
        name="rmsnorm",
    )(x, g.reshape(1, d))


def _proj_kernel(x_ref, w_ref, cs_ref, o_ref):
    x = x_ref[...]
    n = w_ref.shape[1]
    for c in range(n // PROJ_CHUNK):
        sl = slice(c * PROJ_CHUNK, (c + 1) * PROJ_CHUNK)
        o_ref[:, sl] = (_dot(x, w_ref[:, sl]) * cs_ref[:, sl]).astype(o_ref.dtype)


def _proj_gate_kernel(x_ref, w_ref, cs_ref, wf_ref, o_ref, f_ref):
    _proj_kernel(x_ref, w_ref, cs_ref, o_ref)
    f_ref[...] = _dot(x_ref[...], wf_ref[...])


def _proj(xn, w, colscale, w_gate=None):
    m, k = xn.shape
    n = w.shape[1]
    in_specs = [pl.BlockSpec((ROW_TILE, k), lambda i: (i, 0)),
                pl.BlockSpec((k, n), lambda i: (0, 0)),
                pl.BlockSpec((1, n), lambda i: (0, 0))]
    out_specs = [pl.BlockSpec((ROW_TILE, n), lambda i: (i, 0))]
    out_shape = [jax.ShapeDtypeStruct((m, n), BF16)]
    args = [xn, w, colscale]
    body = _proj_kernel
    if w_gate is not None:
        in_specs.append(pl.BlockSpec((k, LANES), lambda i: (0, 0)))
        out_specs.append(pl.BlockSpec((ROW_TILE, LANES), lambda i: (i, 0)))
        out_shape.append(jax.ShapeDtypeStruct((m, LANES), F32))
        args.append(w_gate)
        body = _proj_gate_kernel
    return pl.pallas_call(
        body,
        grid=(m // ROW_TILE,),
        in_specs=in_specs,
        out_specs=out_specs,
        out_shape=out_shape,
        compiler_params=_params(("parallel",), 48),
        name="qkv_proj",
    )(*args)


def _softplus2(y):
    return jnp.maximum(y, 0.0) + jnp.log(1.0 + jnp.exp2(-jnp.abs(y))) * LOG2E


def _gate_cumsum_kernel(f_ref, b_ref, tri_ref, o_ref):
    n_blocks = f_ref.shape[1] // CUM_BLOCK
    tri = tri_ref[...]

    def body(r, carry):
        rows = pl.ds(pl.multiple_of(r * CUM_BLOCK, CUM_BLOCK), CUM_BLOCK)
        logit2 = (f_ref[0, rows, :] + b_ref[...]) * LOG2E
        ls = -_softplus2(-logit2)
        hi = ls.astype(BF16)
        rem = ls - hi.astype(F32)
        mid = rem.astype(BF16)
        lo = (rem - mid.astype(F32)).astype(BF16)
        out = _dot(tri, hi) + _dot(tri, mid) + _dot(tri, lo) + carry
        o_ref[0, rows, :] = out
        return out[CUM_BLOCK - 1:CUM_BLOCK, :]

    lax.fori_loop(0, n_blocks, body, jnp.zeros((1, LANES), F32))


def _gate_cumsum(f_logit, b_gate):
    b, s, _ = f_logit.shape
    idx = jnp.arange(CUM_BLOCK)
    tri = (idx[None, :] <= idx[:, None]).astype(BF16)
    return pl.pallas_call(
        _gate_cumsum_kernel,
        grid=(b,),
        in_specs=[pl.BlockSpec((1, s, LANES), lambda i: (i, 0, 0)),
                  pl.BlockSpec((1, LANES), lambda i: (0, 0)),
                  pl.BlockSpec((CUM_BLOCK, CUM_BLOCK), lambda i: (0, 0))],
        out_specs=pl.BlockSpec((1, s, LANES), lambda i: (i, 0, 0)),
        out_shape=jax.ShapeDtypeStruct((b, s, LANES), F32),
        compiler_params=_params(("parallel",), 32),
        name="gate_cumsum",
    )(f_logit, b_gate, tri)


def _stack_heads(q):
    lane = lax.broadcasted_iota(jnp.int32, q.shape, 1)
    zero = jnp.zeros_like(q)
    return jnp.concatenate([jnp.where(lane < HEAD_DIM, q, zero),
                            jnp.where(lane >= HEAD_DIM, q, zero)], axis=0)


def _query_pos(shape):
    return lax.broadcasted_iota(jnp.int32, shape, 0) & (TQ - 1)


def _merge_heads(top, bottom):
    lane = lax.broadcasted_iota(jnp.int32, top.shape, 1)
    return jnp.where(lane < HEAD_DIM, top, bottom)


def _fox_kernel(q_ref, k_ref, v_ref, c_ref, o_ref, cq_ref, m_ref, acc_ref):
    n_q = q_ref.shape[1] // TQ
    sub = TQ // LANES
    ksub = TK // LANES

    def q_tile(qi, carry):
        r0 = pl.multiple_of(qi * TQ, TQ)
        qs = _stack_heads(q_ref[0, pl.ds(r0, TQ), :])
        for h in range(HEADS_PER_STEP):
            for g in range(sub):
                row = c_ref[0, 0, h, pl.ds(qi * sub + g, 1), :]
                cq_ref[pl.ds(h * TQ + g * LANES, LANES), :] = (
                    jnp.broadcast_to(row, (LANES, LANES)).T)
        m_ref[...] = jnp.full(m_ref.shape, -jnp.inf, F32)
        acc_ref[...] = jnp.zeros(acc_ref.shape, F32)

        def kv_step(j, masked):
            c0 = pl.multiple_of(j * TK, TK)
            k = k_ref[0, pl.ds(c0, TK), :]
            v = v_ref[0, pl.ds(c0, TK), :]
            s = _dot_nt(qs, k)
            cq = cq_ref[...]
            cols = []
            for g in range(ksub):
                sg = s[:, g * LANES:(g + 1) * LANES] + cq
                ck0 = c_ref[0, 0, 0, pl.ds(j * ksub + g, 1), :]
                ck1 = c_ref[0, 0, 1, pl.ds(j * ksub + g, 1), :]
                cols.append(jnp.concatenate([sg[:TQ] - ck0, sg[TQ:] - ck1], axis=0))
            s = jnp.concatenate(cols, axis=1)
            if masked:
                key = lax.broadcasted_iota(jnp.int32, s.shape, 1)
                s = jnp.where(key <= _query_pos(s.shape), s, -jnp.inf)
            m_old = m_ref[...]
            m_new = jnp.maximum(m_old, jnp.max(s, axis=-1, keepdims=True))
            p = jnp.exp2(s - m_new).astype(BF16)
            alpha = jnp.exp2(m_old - m_new)
            lane = lax.broadcasted_iota(jnp.int32, v.shape, 1)
            one = jnp.ones_like(v)
            pv = jnp.concatenate(
                [_dot(p[:TQ], jnp.where(lane < HEAD_DIM, v, one)),
                 _dot(p[TQ:], jnp.where(lane >= HEAD_DIM, v, one))], axis=0)
            acc_ref[...] = acc_ref[...] * alpha + pv
            m_ref[...] = m_new

        def loop_body(j, c):
            kv_step(j, False)
            return c

        lax.fori_loop(0, qi, loop_body, 0)
        kv_step(qi, True)

        acc = acc_ref[...]
        top, bottom = acc[:TQ], acc[TQ:]
        lane = lax.broadcasted_iota(jnp.int32, top.shape, 1)
        l_top = jnp.sum(jnp.where(lane >= HEAD_DIM, top, 0.0), axis=-1, keepdims=True)
        l_bot = jnp.sum(jnp.where(lane < HEAD_DIM, bottom, 0.0), axis=-1, keepdims=True)
        out = _merge_heads(top * (HEAD_DIM / l_top), bottom * (HEAD_DIM / l_bot))
        o_ref[0, pl.ds(r0, TQ), :] = out.astype(o_ref.dtype)
        return carry

    lax.fori_loop(0, n_q, q_tile, 0)


def _sb_kernel(q_ref, k_ref, v_ref, u_ref, o_ref, r_ref, acc_ref):
    n_q = q_ref.shape[1] // TQ

    def q_tile(qi, carry):
        r0 = pl.multiple_of(qi * TQ, TQ)
        qs = _stack_heads(q_ref[0, pl.ds(r0, TQ), :])
        r_ref[...] = jnp.zeros(r_ref.shape, F32)
        acc_ref[...] = jnp.zeros(acc_ref.shape, F32)

        def kv_step(j, masked):
            c0 = pl.multiple_of(j * TK, TK)
            k = k_ref[0, pl.ds(c0, TK), :]
            v = v_ref[0, pl.ds(c0, TK), :]
            z = _dot_nt(qs, k)
            sp = _softplus2(z)
            if masked:
                key = lax.broadcasted_iota(jnp.int32, z.shape, 1)
                valid = key < _query_pos(z.shape)
                sp = jnp.where(valid, sp, 0.0)
            hi = sp.astype(BF16)
            lo = (sp - hi.astype(F32)).astype(BF16)
            u = u_ref[...]
            incl = _dot(hi, u) + _dot(lo, u)
            a = jnp.exp2(z - incl - r_ref[...])
            if masked:
                a = jnp.where(valid, a, 0.0)
            acc_ref[...] += _dot(a.astype(BF16), v)
            r_ref[...] += jnp.sum(sp, axis=-1, keepdims=True)

        kv_step(qi, True)

        def loop_body(it, c):
            kv_step(qi - 1 - it, False)
            return c

        lax.fori_loop(0, qi, loop_body, 0)

        acc = acc_ref[...]
        o_ref[0, pl.ds(r0, TQ), :] = _merge_heads(acc[:TQ], acc[TQ:]).astype(o_ref.dtype)
        return carry

    lax.fori_loop(0, n_q, q_tile, 0)


def _head_specs(s):
    def spec(offset):
        return pl.BlockSpec((1, s, LANES), lambda b, h: (b, 0, offset + h))
    return [spec(0), spec(N_HEAD_PAIRS), spec(2 * N_HEAD_PAIRS)]


def _fox_attention(qkv, c_rows):
    b, s, _ = qkv.shape
    return pl.pallas_call(
        _fox_kernel,
        grid=(b, N_HEAD_PAIRS),
        in_specs=_head_specs(s) + [
            pl.BlockSpec((1, 1, HEADS_PER_STEP, s // LANES, LANES),
                         lambda b, h: (b, h, 0, 0, 0))],
        out_specs=pl.BlockSpec((1, s, LANES), lambda b, h: (b, 0, h)),
        out_shape=jax.ShapeDtypeStruct((b, s, D_MODEL), BF16),
        scratch_shapes=[pltpu.VMEM((HEADS_PER_STEP * TQ, LANES), F32),
                        pltpu.VMEM((HEADS_PER_STEP * TQ, 1), F32),
                        pltpu.VMEM((HEADS_PER_STEP * TQ, LANES), F32)],
        compiler_params=_params(("parallel", "parallel"), 48),
        name="fox_attention",
    )(qkv, qkv, qkv, c_rows)


def _sb_attention(qkv):
    b, s, _ = qkv.shape
    idx = jnp.arange(TK)
    upper = (idx[:, None] >= idx[None, :]).astype(BF16)
    return pl.pallas_call(
        _sb_kernel,
        grid=(b, N_HEAD_PAIRS),
        in_specs=_head_specs(s) + [pl.BlockSpec((TK, TK), lambda b, h: (0, 0))],
        out_specs=pl.BlockSpec((1, s, LANES), lambda b, h: (b, 0, h)),
        out_shape=jax.ShapeDtypeStruct((b, s, D_MODEL), BF16),
        scratch_shapes=[pltpu.VMEM((HEADS_PER_STEP * TQ, 1), F32),
                        pltpu.VMEM((HEADS_PER_STEP * TQ, LANES), F32)],
        compiler_params=_params(("parallel", "parallel"), 48),
        name="sb_attention",
    )(qkv, qkv, qkv, upper)


def _mm_resid_norm_kernel(a_ref, w_ref, h_ref, g_ref, hs_ref, xn_ref):
    h = _dot(a_ref[...], w_ref[...]) + h_ref[...]
    hs_ref[...] = h
    xn_ref[...] = _rms(h, g_ref[...]).astype(xn_ref.dtype)


def _mm_resid_final_kernel(a_ref, w_ref, h_ref, g_ref, o_ref):
    h = _dot(a_ref[...], w_ref[...]) + h_ref[...]
    o_ref[...] = _rms(h, g_ref[...]).astype(o_ref.dtype)


def _mm_resid_norm(a, w, hs, g, final=False):
    m, k = a.shape
    d = w.shape[1]
    row = lambda i: (i, 0)
    in_specs = [pl.BlockSpec((ROW_TILE, k), row),
                pl.BlockSpec((k, d), lambda i: (0, 0)),
                pl.BlockSpec((ROW_TILE, d), row),
                pl.BlockSpec((1, d), lambda i: (0, 0))]
    if final:
        body = _mm_resid_final_kernel
        out_specs = pl.BlockSpec((ROW_TILE, d), row)
        out_shape = jax.ShapeDtypeStruct((m, d), F32)
    else:
        body = _mm_resid_norm_kernel
        out_specs = [pl.BlockSpec((ROW_TILE, d), row), pl.BlockSpec((ROW_TILE, d), row)]
        out_shape = [jax.ShapeDtypeStruct((m, d), F32), jax.ShapeDtypeStruct((m, d), BF16)]
    return pl.pallas_call(
        body,
        grid=(m // ROW_TILE,),
        in_specs=in_specs,
        out_specs=out_specs,
        out_shape=out_shape,
        compiler_params=_params(("parallel",), 48),
        name="matmul_residual_norm",
    )(a, w, hs, g.reshape(1, d))


def _ffn_up_kernel(x_ref, w_ref, wc_ref, bc_ref, o_ref, carry_ref, *, tiles_per_seq):
    i = pl.program_id(0)

    @pl.when(i % tiles_per_seq == 0)
    def _():
        carry_ref[...] = jnp.zeros(carry_ref.shape, F32)

    x = x_ref[...]
    tm = x.shape[0]
    row = lax.broadcasted_iota(jnp.int32, (tm, 2 * FF_CHUNK), 0)
    for c in range(N_FF_CHUNKS):
        h = _dot(x, w_ref[c])
        prev = carry_ref[c]
        p1 = prev[7:8]
        p2 = prev[6:7]
        h1 = jnp.where(row == 0, p1, pltpu.roll(h, 1, 0))
        h2 = jnp.where(row == 0, p2, jnp.where(row == 1, p1, pltpu.roll(h, 2, 0)))
        wc = wc_ref[c]
        hc = bc_ref[c] + wc[0:1] * h2 + wc[1:2] * h1 + wc[2:3] * h
        u = hc[:, :FF_CHUNK]
        g = hc[:, FF_CHUNK:]
        act = g * u / (1.0 + jnp.exp(-g))
        o_ref[:, c * FF_CHUNK:(c + 1) * FF_CHUNK] = act.astype(o_ref.dtype)
        carry_ref[c] = h[tm - 8:tm]


def _ffn_up(xn, w_chunks, wc_chunks, bc_chunks, seq_len):
    m, k = xn.shape
    kern = functools.partial(_ffn_up_kernel, tiles_per_seq=seq_len // ROW_TILE)
    return pl.pallas_call(
        kern,
        grid=(m // ROW_TILE,),
        in_specs=[pl.BlockSpec((ROW_TILE, k), lambda i: (i, 0)),
                  pl.BlockSpec((N_FF_CHUNKS, k, 2 * FF_CHUNK), lambda i: (0, 0, 0)),
                  pl.BlockSpec((N_FF_CHUNKS, CONV_WIDTH, 2 * FF_CHUNK), lambda i: (0, 0, 0)),
                  pl.BlockSpec((N_FF_CHUNKS, 1, 2 * FF_CHUNK), lambda i: (0, 0, 0))],
        out_specs=pl.BlockSpec((ROW_TILE, D_FF), lambda i: (i, 0)),
        out_shape=jax.ShapeDtypeStruct((m, D_FF), BF16),
        scratch_shapes=[pltpu.VMEM((N_FF_CHUNKS, 8, 2 * FF_CHUNK), F32)],
        compiler_params=_params(("arbitrary",), 56),
        name="ffn_up_conv_gate",
    )(xn, w_chunks, wc_chunks, bc_chunks)


def _chunk_columns(t):
    r = t.shape[0]
    t = t.reshape(r, 2, N_FF_CHUNKS, FF_CHUNK)
    return t.transpose(2, 0, 1, 3).reshape(N_FF_CHUNKS, r, 2 * FF_CHUNK)


def kernel(x, attn_norm, ffn_norm, final_norm, fox_w_qkvf, fox_b_f, fox_w_o,
           sb_w_qkv, sb_w_o, ffn_w_up, ffn_w_conv, ffn_b_conv, ffn_w_down):
    b, s, d = x.shape
    m = b * s
    assert d == D_MODEL and s % TQ == 0 and s % ROW_TILE == 0

    colscale = jnp.concatenate([jnp.full((1, d), ATTN_SCALE * LOG2E, F32),
                                jnp.ones((1, 2 * d), F32)], axis=1)

    hs = x.reshape(m, d)
    xn = _rmsnorm(hs, attn_norm[0], BF16)
    out = None
    for i in range(DEPTH):
        j = i // 2
        if i % 2 == 0:
            w = fox_w_qkvf[j]
            w_gate = jnp.pad(w[:, 3 * d:], ((0, 0), (0, LANES - N_HEADS))).astype(BF16)
            qkv, f_logit = _proj(xn, w[:, :3 * d].astype(BF16), colscale, w_gate)
            b_gate = jnp.pad(fox_b_f[j], (0, LANES - N_HEADS)).reshape(1, LANES)
            c2 = _gate_cumsum(f_logit.reshape(b, s, LANES), b_gate)
            c_rows = c2[:, :, :N_HEADS].transpose(0, 2, 1).reshape(
                b, N_HEAD_PAIRS, HEADS_PER_STEP, s // LANES, LANES)
            o = _fox_attention(qkv.reshape(b, s, 3 * d), c_rows)
            w_o = fox_w_o[j]
        else:
            qkv = _proj(xn, sb_w_qkv[j].astype(BF16), colscale)[0]
            o = _sb_attention(qkv.reshape(b, s, 3 * d))
            w_o = sb_w_o[j]
        hs, xn = _mm_resid_norm(o.reshape(m, d), w_o.astype(BF16), hs, ffn_norm[i])
        act = _ffn_up(xn,
                      _chunk_columns(ffn_w_up[i]).astype(BF16),
                      _chunk_columns(ffn_w_conv[i]),
                      _chunk_columns(ffn_b_conv[i].reshape(1, -1)),
                      s)
        w_down = ffn_w_down[i].astype(BF16)
        if i + 1 < DEPTH:
            hs, xn = _mm_resid_norm(act, w_down, hs, attn_norm[i + 1])
        else:
            out = _mm_resid_norm(act, w_down, hs, final_norm, final=True)
    return out.reshape(b, s, d)
```

```python
import functools
import math

import jax
import jax.numpy as jnp
from jax import lax
from jax.experimental import pallas as pl
from jax.experimental.pallas import tpu as pltpu

D_MODEL = 1024
N_HEADS = 16
HEAD_DIM = D_MODEL // N_HEADS
D_FF = 2816
CONV_WIDTH = 3
DEPTH = 4
NORM_EPS = 1e-6
ATTN_SCALE = HEAD_DIM ** -0.5
LOG2E = math.log2(math.e)

F32 = jnp.float32
BF16 = jnp.bfloat16

LANES = 128
HEADS_PER_PAIR = LANES // HEAD_DIM
N_HEAD_PAIRS = N_HEADS // HEADS_PER_PAIR
ROW_TILE = 512
PROJ_CHUNK = 512
FF_CHUNK = 256
N_FF_CHUNKS = D_FF // FF_CHUNK
TQ = 512
TK = 512
VT_BLOCK = 256
CUM_BLOCK = 128
GATE_PARTS = 3
ONES_LANE = GATE_PARTS * N_HEADS
MIB = 2 ** 20


def _params(semantics, vmem_mib):
    return pltpu.CompilerParams(dimension_semantics=semantics,
                                vmem_limit_bytes=vmem_mib * MIB)


def _dot(a, b):
    return jnp.dot(a, b, preferred_element_type=F32)


def _dot_nt(a, b):
    return lax.dot_general(a, b, (((1,), (1,)), ((), ())), preferred_element_type=F32)


def _split_bf16(x, parts):
    pieces = []
    for _ in range(parts - 1):
        p = x.astype(BF16)
        pieces.append(p)
        x = x - p.astype(F32)
    pieces.append(x.astype(BF16))
    return pieces


def _rms(h, g):
    ms = jnp.mean(h * h, axis=-1, keepdims=True)
    return h * lax.rsqrt(ms + NORM_EPS) * g


def _rmsnorm_kernel(x_ref, g_ref, o_ref):
    o_ref[...] = _rms(x_ref[...], g_ref[...]).astype(o_ref.dtype)


def _rmsnorm(x, g, out_dtype):
    m, d = x.shape
    return pl.pallas_call(
        _rmsnorm_kernel,
        grid=(m // ROW_TILE,),
        in_specs=[pl.BlockSpec((ROW_TILE, d), lambda i: (i, 0)),
                  pl.BlockSpec((1, d), lambda i: (0, 0))],
        out_specs=pl.BlockSpec((ROW_TILE, d), lambda i: (i, 0)),
        out_shape=jax.ShapeDtypeStruct((m, d), out_dtype),
        compiler_params=_params(("parallel",), 32),
        name="rmsnorm",
    )(x, g.reshape(1, d))


def _proj_kernel(x_ref, w_ref, cs_ref, o_ref):
    x = x_ref[...]
    n = w_ref.shape[1]
    for c in range(n // PROJ_CHUNK):
        sl = slice(c * PROJ_CHUNK, (c + 1) * PROJ_CHUNK)
        o_ref[:, sl] = (_dot(x, w_ref[:, sl]) * cs_ref[:, sl]).astype(o_ref.dtype)


def _proj_gate_kernel(x_ref, w_ref, cs_ref, wf_ref, o_ref, f_ref):
    _proj_kernel(x_ref, w_ref, cs_ref, o_ref)
    f_ref[...] = _dot(x_ref[...], wf_ref[...])


def _proj(xn, w, colscale, w_gate=None):
    m, k = xn.shape
    n = w.shape[1]
    in_specs = [pl.BlockSpec((ROW_TILE, k), lambda i: (i, 0)),
                pl.BlockSpec((k, n), lambda i: (0, 0)),
                pl.BlockSpec((1, n), lambda i: (0, 0))]
    out_specs = [pl.BlockSpec((ROW_TILE, n), lambda i: (i, 0))]
    out_shape = [jax.ShapeDtypeStruct((m, n), BF16)]
    args = [xn, w, colscale]
    body = _proj_kernel
    if w_gate is not None:
        in_specs.append(pl.BlockSpec((k, LANES), lambda i: (0, 0)))
        out_specs.append(pl.BlockSpec((ROW_TILE, LANES), lambda i: (i, 0)))
        out_shape.append(jax.ShapeDtypeStruct((m, LANES), F32))
        args.append(w_gate)
        body = _proj_gate_kernel
    return pl.pallas_call(
        body,
        grid=(m // ROW_TILE,),
        in_specs=in_specs,
        out_specs=out_specs,
        out_shape=out_shape,
        compiler_params=_params(("parallel",), 48),
        name="qkv_proj",
    )(*args)


def _softplus2(y):
    return jnp.maximum(y, 0.0) + jnp.log(1.0 + jnp.exp2(-jnp.abs(y))) * LOG2E


def _gate_cumsum_kernel(f_ref, b_ref, tri_ref, place_ref, o_ref):
    n_blocks = f_ref.shape[1] // CUM_BLOCK
    tri = tri_ref[...]
    lane = lax.broadcasted_iota(jnp.int32, (CUM_BLOCK, LANES), 1)
    ones = jnp.where((lane >= ONES_LANE) & (lane < ONES_LANE + GATE_PARTS), 1.0, 0.0)

    def body(r, carry):
        rows = pl.ds(pl.multiple_of(r * CUM_BLOCK, CUM_BLOCK), CUM_BLOCK)
        logit2 = (f_ref[0, rows, :] + b_ref[...]) * LOG2E
        ls = -_softplus2(-logit2)
        c2 = carry
        for piece in _split_bf16(ls, GATE_PARTS):
            c2 = c2 + _dot(tri, piece)
        operand = ones
        for i, piece in enumerate(_split_bf16(c2, GATE_PARTS)):
            operand = operand + _dot(piece, place_ref[i])
        o_ref[0, rows, :] = operand.astype(o_ref.dtype)
        return c2[CUM_BLOCK - 1:CUM_BLOCK, :]

    lax.fori_loop(0, n_blocks, body, jnp.zeros((1, LANES), F32))


def _gate_operand(f_logit, b_gate):
    b, s, _ = f_logit.shape
    idx = jnp.arange(CUM_BLOCK)
    tri = (idx[None, :] <= idx[:, None]).astype(BF16)
    lane = jnp.arange(LANES)
    place = jnp.stack([((lane[:, None] < N_HEADS) & (lane[None, :] == i * N_HEADS + lane[:, None]))
                       for i in range(GATE_PARTS)]).astype(BF16)
    return pl.pallas_call(
        _gate_cumsum_kernel,
        grid=(b,),
        in_specs=[pl.BlockSpec((1, s, LANES), lambda i: (i, 0, 0)),
                  pl.BlockSpec((1, LANES), lambda i: (0, 0)),
                  pl.BlockSpec((CUM_BLOCK, CUM_BLOCK), lambda i: (0, 0)),
                  pl.BlockSpec((GATE_PARTS, LANES, LANES), lambda i: (0, 0, 0))],
        out_specs=pl.BlockSpec((1, s, LANES), lambda i: (i, 0, 0)),
        out_shape=jax.ShapeDtypeStruct((b, s, LANES), BF16),
        compiler_params=_params(("parallel",), 32),
        name="gate_cumsum",
    )(f_logit, b_gate, tri, place)


def _stack_heads(q):
    lane = lax.broadcasted_iota(jnp.int32, q.shape, 1)
    zero = jnp.zeros_like(q)
    return jnp.concatenate([jnp.where(lane < HEAD_DIM, q, zero),
                            jnp.where(lane >= HEAD_DIM, q, zero)], axis=0)


def _head_cols(x, col_lo, axis):
    if col_lo == 0:
        return x
    idx = [slice(None)] * x.ndim
    parts = []
    for h in range(HEADS_PER_PAIR):
        idx[axis] = slice(h * TQ + col_lo, (h + 1) * TQ)
        parts.append(x[tuple(idx)])
    return jnp.concatenate(parts, axis=axis)


def _store_head_cols(ref, col_lo, value):
    if col_lo == 0:
        ref[...] = value
        return
    w = TQ - col_lo
    for h in range(HEADS_PER_PAIR):
        ref[:, h * TQ + col_lo:(h + 1) * TQ] = value[:, h * w:(h + 1) * w]


def _causal(shape, strict):
    key = lax.broadcasted_iota(jnp.int32, shape, 0)
    query = lax.broadcasted_iota(jnp.int32, shape, 1) & (shape[1] // HEADS_PER_PAIR - 1)
    return key < query if strict else key <= query


def _transpose_values(v_ref, store):
    row = lax.broadcasted_iota(jnp.int32, (LANES, LANES), 0)
    col = lax.broadcasted_iota(jnp.int32, (LANES, LANES), 1)
    eye = (row == col).astype(BF16)

    def body(j, carry):
        rows = pl.ds(pl.multiple_of(j * VT_BLOCK, VT_BLOCK), VT_BLOCK)
        store(j, _dot_nt(eye, v_ref[0, rows, :]).astype(BF16))
        return carry

    lax.fori_loop(0, v_ref.shape[1] // VT_BLOCK, body, 0)


def _sweep(qi, step, reverse):
    per_block = TK // VT_BLOCK
    n_diag = TQ // VT_BLOCK

    def full(j):
        step(pl.multiple_of(j * TK, TK), TK, [j * per_block + i for i in range(per_block)], 0, False)

    def diag(d):
        start = pl.multiple_of(qi * TQ + d * VT_BLOCK, VT_BLOCK)
        step(start, VT_BLOCK, [qi * n_diag + d], d * VT_BLOCK, True)

    if reverse:
        for d in reversed(range(n_diag)):
            diag(d)
        lax.fori_loop(0, qi, lambda it, c: (full(qi - 1 - it), c)[1], 0)
    else:
        lax.fori_loop(0, qi, lambda j, c: (full(j), c)[1], 0)
        for d in range(n_diag):
            diag(d)


def _store_output(o_ref, acc_t):
    out_t = jnp.concatenate([acc_t[:HEAD_DIM, :TQ], acc_t[HEAD_DIM:, TQ:]], axis=0)
    o_ref[0] = out_t.T.astype(o_ref.dtype)


def _fox_kernel(q_ref, k_ref, v_ref, gate_ref, perm_ref, o_ref, vt_ref, qa_ref, m_ref, acc_ref):
    pair = pl.program_id(1)
    qi = pl.program_id(2)

    @pl.when(qi == 0)
    def _():
        row = lax.broadcasted_iota(jnp.int32, (LANES, VT_BLOCK), 0)

        def store(j, vt):
            one = jnp.ones_like(vt)
            vt_ref[0, j] = jnp.where(row < HEAD_DIM, vt, one)
            vt_ref[1, j] = jnp.where(row >= HEAD_DIM, vt, one)

        _transpose_values(v_ref, store)

    gate_q = gate_ref[0, pl.ds(pl.multiple_of(qi * TQ, TQ), TQ), :]
    lane = lax.broadcasted_iota(jnp.int32, (TQ, LANES), 1)
    q = q_ref[0]
    zero = jnp.zeros_like(q)
    for hh in range(HEADS_PER_PAIR):
        h = HEADS_PER_PAIR * pair + hh
        minus = jnp.zeros((TQ, LANES), F32)
        for i in range(GATE_PARTS):
            minus = jnp.where(lane == i * N_HEADS + h, -1.0, minus)
        gate_lanes = (_dot(gate_q, perm_ref[h]) + minus).astype(BF16)
        feat = (lane < HEAD_DIM) if hh == 0 else (lane >= HEAD_DIM)
        qa_ref[hh * TQ:(hh + 1) * TQ, :] = jnp.concatenate(
            [jnp.where(feat, q, zero), gate_lanes], axis=1)
    m_ref[...] = jnp.full(m_ref.shape, -jnp.inf, F32)
    acc_ref[...] = jnp.zeros(acc_ref.shape, F32)

    def step(key_start, n_keys, vt_blocks, col_lo, masked):
        rows = pl.ds(key_start, n_keys)
        w = TQ - col_lo
        k_aug = jnp.concatenate([k_ref[0, rows, :], gate_ref[0, rows, :]], axis=1)
        s = _dot_nt(k_aug, _head_cols(qa_ref[...], col_lo, 0))
        if masked:
            s = jnp.where(_causal(s.shape, strict=False), s, -jnp.inf)
        m_old = _head_cols(m_ref[...], col_lo, 1)
        m_new = jnp.maximum(m_old, jnp.max(s, axis=0, keepdims=True))
        prob = jnp.exp2(s - m_new).astype(BF16)
        alpha = jnp.exp2(m_old - m_new)
        pv = jnp.concatenate(
            [_dot(jnp.concatenate([vt_ref[hh, j] for j in vt_blocks], axis=1),
                  prob[:, hh * w:(hh + 1) * w]) for hh in range(HEADS_PER_PAIR)], axis=1)
        _store_head_cols(acc_ref, col_lo, _head_cols(acc_ref[...], col_lo, 1) * alpha + pv)
        _store_head_cols(m_ref, col_lo, m_new)

    _sweep(qi, step, reverse=False)

    acc = acc_ref[...]
    denom = jnp.concatenate(
        [jnp.broadcast_to(acc[HEAD_DIM:HEAD_DIM + 1, :TQ], (LANES, TQ)),
         jnp.broadcast_to(acc[0:1, TQ:], (LANES, TQ))], axis=1)
    _store_output(o_ref, acc / denom)


def _sb_kernel(q_ref, k_ref, v_ref, tt_ref, o_ref, vt_ref, qs_ref, r_ref, acc_ref):
    qi = pl.program_id(2)

    @pl.when(qi == 0)
    def _():
        def store(j, vt):
            vt_ref[j] = vt

        _transpose_values(v_ref, store)

    qs_ref[...] = _stack_heads(q_ref[0])
    r_ref[...] = jnp.zeros(r_ref.shape, F32)
    acc_ref[...] = jnp.zeros(acc_ref.shape, F32)

    def step(key_start, n_keys, vt_blocks, col_lo, masked):
        rows = pl.ds(key_start, n_keys)
        z = _dot_nt(k_ref[0, rows, :], _head_cols(qs_ref[...], col_lo, 0))
        sp = _softplus2(z)
        if masked:
            valid = _causal(z.shape, strict=True)
            sp = jnp.where(valid, sp, 0.0)
        r = _head_cols(r_ref[...], col_lo, 1)
        n_sub = n_keys // CUM_BLOCK
        weights = [None] * n_sub
        for g in reversed(range(n_sub)):
            sub = slice(g * CUM_BLOCK, (g + 1) * CUM_BLOCK)
            hi, lo = _split_bf16(sp[sub], 2)
            incl = _dot(tt_ref[...], jnp.concatenate([hi, lo], axis=0))
            a = jnp.exp2(z[sub] - incl - r)
            if masked:
                a = jnp.where(valid[sub], a, 0.0)
            weights[g] = a.astype(BF16)
            r = r + jnp.sum(sp[sub], axis=0, keepdims=True)
        _store_head_cols(r_ref, col_lo, r)
        vt = jnp.concatenate([vt_ref[j] for j in vt_blocks], axis=1)
        pv = _dot(vt, jnp.concatenate(weights, axis=0))
        _store_head_cols(acc_ref, col_lo, _head_cols(acc_ref[...], col_lo, 1) + pv)

    _sweep(qi, step, reverse=True)
    _store_output(o_ref, acc_ref[...])


def _attention_call(body, name, qkv, extra_inputs, extra_specs, scratch_shapes):
    b, s, _ = qkv.shape
    return pl.pallas_call(
        body,
        grid=(b, N_HEAD_PAIRS, s // TQ),
        in_specs=[pl.BlockSpec((1, TQ, LANES), lambda b, p, i: (b, i, p)),
                  pl.BlockSpec((1, s, LANES), lambda b, p, i: (b, 0, N_HEAD_PAIRS + p)),
                  pl.BlockSpec((1, s, LANES), lambda b, p, i: (b, 0, 2 * N_HEAD_PAIRS + p))
                  ] + extra_specs,
        out_specs=pl.BlockSpec((1, TQ, LANES), lambda b, p, i: (b, i, p)),
        out_shape=jax.ShapeDtypeStruct((b, s, D_MODEL), BF16),
        scratch_shapes=scratch_shapes,
        compiler_params=_params(("parallel", "parallel", "arbitrary"), 48),
        name=name,
    )(qkv, qkv, qkv, *extra_inputs)


def _fox_attention(qkv, gate):
    b, s, _ = qkv.shape
    lane = jnp.arange(LANES)
    perm = jnp.stack([
        sum(((lane[:, None] == i * N_HEADS + h) & (lane[None, :] == ONES_LANE + i))
            for i in range(GATE_PARTS))
        for h in range(N_HEADS)]).astype(BF16)
    return _attention_call(
        _fox_kernel, "fox_attention", qkv, [gate, perm],
        [pl.BlockSpec((1, s, LANES), lambda b, p, i: (b, 0, 0)),
         pl.BlockSpec((N_HEADS, LANES, LANES), lambda b, p, i: (0, 0, 0))],
        [pltpu.VMEM((HEADS_PER_PAIR, s // VT_BLOCK, LANES, VT_BLOCK), BF16),
         pltpu.VMEM((HEADS_PER_PAIR * TQ, 2 * LANES), BF16),
         pltpu.VMEM((1, HEADS_PER_PAIR * TQ), F32),
         pltpu.VMEM((LANES, HEADS_PER_PAIR * TQ), F32)])


def _sb_attention(qkv):
    b, s, _ = qkv.shape
    idx = jnp.arange(CUM_BLOCK)
    suffix = (idx[None, :] >= idx[:, None]).astype(BF16)
    tt = jnp.concatenate([suffix, suffix], axis=1)
    return _attention_call(
        _sb_kernel, "sb_attention", qkv, [tt],
        [pl.BlockSpec((CUM_BLOCK, 2 * CUM_BLOCK), lambda b, p, i: (0, 0))],
        [pltpu.VMEM((s // VT_BLOCK, LANES, VT_BLOCK), BF16),
         pltpu.VMEM((HEADS_PER_PAIR * TQ, LANES), BF16),
         pltpu.VMEM((1, HEADS_PER_PAIR * TQ), F32),
         pltpu.VMEM((LANES, HEADS_PER_PAIR * TQ), F32)])


def _mm_resid_norm_kernel(a_ref, w_ref, h_ref, g_ref, hs_ref, xn_ref):
    h = _dot(a_ref[...], w_ref[...]) + h_ref[...]
    hs_ref[...] = h
    xn_ref[...] = _rms(h, g_ref[...]).astype(xn_ref.dtype)


def _mm_resid_final_kernel(a_ref, w_ref, h_ref, g_ref, o_ref):
    h = _dot(a_ref[...], w_ref[...]) + h_ref[...]
    o_ref[...] = _rms(h, g_ref[...]).astype(o_ref.dtype)


def _mm_resid_norm(a, w, hs, g, final=False):
    m, k = a.shape
    d = w.shape[1]
    row = lambda i: (i, 0)
    in_specs = [pl.BlockSpec((ROW_TILE, k), row),
                pl.BlockSpec((k, d), lambda i: (0, 0)),
                pl.BlockSpec((ROW_TILE, d), row),
                pl.BlockSpec((1, d), lambda i: (0, 0))]
    if final:
        body = _mm_resid_final_kernel
        out_specs = pl.BlockSpec((ROW_TILE, d), row)
        out_shape = jax.ShapeDtypeStruct((m, d), F32)
    else:
        body = _mm_resid_norm_kernel
        out_specs = [pl.BlockSpec((ROW_TILE, d), row), pl.BlockSpec((ROW_TILE, d), row)]
        out_shape = [jax.ShapeDtypeStruct((m, d), F32), jax.ShapeDtypeStruct((m, d), BF16)]
    return pl.pallas_call(
        body,
        grid=(m // ROW_TILE,),
        in_specs=in_specs,
        out_specs=out_specs,
        out_shape=out_shape,
        compiler_params=_params(("parallel",), 48),
        name="matmul_residual_norm",
    )(a, w, hs, g.reshape(1, d))


def _ffn_up_kernel(x_ref, w_ref, wc_ref, bc_ref, o_ref, carry_ref, *, tiles_per_seq):
    i = pl.program_id(0)

    @pl.when(i % tiles_per_seq == 0)
    def _():
        carry_ref[...] = jnp.zeros(carry_ref.shape, F32)

    x = x_ref[...]
    tm = x.shape[0]
    row = lax.broadcasted_iota(jnp.int32, (tm, 2 * FF_CHUNK), 0)
    for c in range(N_FF_CHUNKS):
        h = _dot(x, w_ref[c])
        prev = carry_ref[c]
        p1 = prev[7:8]
        p2 = prev[6:7]
        h1 = jnp.where(row == 0, p1, pltpu.roll(h, 1, 0))
        h2 = jnp.where(row == 0, p2, jnp.where(row == 1, p1, pltpu.roll(h, 2, 0)))
        wc = wc_ref[c]
        hc = bc_ref[c] + wc[0:1] * h2 + wc[1:2] * h1 + wc[2:3] * h
        u = hc[:, :FF_CHUNK]
        g = hc[:, FF_CHUNK:]
        act = g * u / (1.0 + jnp.exp(-g))
        o_ref[:, c * FF_CHUNK:(c + 1) * FF_CHUNK] = act.astype(o_ref.dtype)
        carry_ref[c] = h[tm - 8:tm]


def _ffn_up(xn, w_chunks, wc_chunks, bc_chunks, seq_len):
    m, k = xn.shape
    kern = functools.partial(_ffn_up_kernel, tiles_per_seq=seq_len // ROW_TILE)
    return pl.pallas_call(
        kern,
        grid=(m // ROW_TILE,),
        in_specs=[pl.BlockSpec((ROW_TILE, k), lambda i: (i, 0)),
                  pl.BlockSpec((N_FF_CHUNKS, k, 2 * FF_CHUNK), lambda i: (0, 0, 0)),
                  pl.BlockSpec((N_FF_CHUNKS, CONV_WIDTH, 2 * FF_CHUNK), lambda i: (0, 0, 0)),
                  pl.BlockSpec((N_FF_CHUNKS, 1, 2 * FF_CHUNK), lambda i: (0, 0, 0))],
        out_specs=pl.BlockSpec((ROW_TILE, D_FF), lambda i: (i, 0)),
        out_shape=jax.ShapeDtypeStruct((m, D_FF), BF16),
        scratch_shapes=[pltpu.VMEM((N_FF_CHUNKS, 8, 2 * FF_CHUNK), F32)],
        compiler_params=_params(("arbitrary",), 56),
        name="ffn_up_conv_gate",
    )(xn, w_chunks, wc_chunks, bc_chunks)


def _chunk_columns(t):
    r = t.shape[0]
    t = t.reshape(r, 2, N_FF_CHUNKS, FF_CHUNK)
    return t.transpose(2, 0, 1, 3).reshape(N_FF_CHUNKS, r, 2 * FF_CHUNK)


def kernel(x, attn_norm, ffn_norm, final_norm, fox_w_qkvf, fox_b_f, fox_w_o,
           sb_w_qkv, sb_w_o, ffn_w_up, ffn_w_conv, ffn_b_conv, ffn_w_down):
    b, s, d = x.shape
    m = b * s
    assert d == D_MODEL and s % TQ == 0 and s % ROW_TILE == 0

    colscale = jnp.concatenate([jnp.full((1, d), ATTN_SCALE * LOG2E, F32),
                                jnp.ones((1, 2 * d), F32)], axis=1)

    hs = x.reshape(m, d)
    xn = _rmsnorm(hs, attn_norm[0], BF16)
    out = None
    for i in range(DEPTH):
        j = i // 2
        if i % 2 == 0:
            w = fox_w_qkvf[j]
            w_gate = jnp.pad(w[:, 3 * d:], ((0, 0), (0, LANES - N_HEADS))).astype(BF16)
            qkv, f_logit = _proj(xn, w[:, :3 * d].astype(BF16), colscale, w_gate)
            b_gate = jnp.pad(fox_b_f[j], (0, LANES - N_HEADS)).reshape(1, LANES)
            gate = _gate_operand(f_logit.reshape(b, s, LANES), b_gate)
            o = _fox_attention(qkv.reshape(b, s, 3 * d), gate)
            w_o = fox_w_o[j]
        else:
            qkv = _proj(xn, sb_w_qkv[j].astype(BF16), colscale)[0]
            o = _sb_attention(qkv.reshape(b, s, 3 * d))
            w_o = sb_w_o[j]
        hs, xn = _mm_resid_norm(o.reshape(m, d), w_o.astype(BF16), hs, ffn_norm[i])
        act = _ffn_up(xn,
                      _chunk_columns(ffn_w_up[i]).astype(BF16),
                      _chunk_columns(ffn_w_conv[i]),
                      _chunk_columns(ffn_b_conv[i].reshape(1, -1)),
                      s)
        w_down = ffn_w_down[i].astype(BF16)
        if i + 1 < DEPTH:
            hs, xn = _mm_resid_norm(act, w_down, hs, attn_norm[i + 1])
        else:
            out = _mm_resid_norm(act, w_down, hs, final_norm, final=True)
    return out.reshape(b, s, d)
```

```python
import functools
import math

import jax
import jax.numpy as jnp
from jax import lax
from jax.experimental import pallas as pl
from jax.experimental.pallas import tpu as pltpu

D_MODEL = 1024
N_HEADS = 16
HEAD_DIM = D_MODEL // N_HEADS
D_FF = 2816
CONV_WIDTH = 3
DEPTH = 4
NORM_EPS = 1e-6
ATTN_SCALE = HEAD_DIM ** -0.5
LOG2E = math.log2(math.e)

F32 = jnp.float32
BF16 = jnp.bfloat16

LANES = 128
SUBLANES = 8
HEADS_PER_PAIR = LANES // HEAD_DIM
N_HEAD_PAIRS = N_HEADS // HEADS_PER_PAIR
ROW_TILE = 512
PROJ_CHUNK = 512
FF_CHUNK = 256
N_FF_CHUNKS = D_FF // FF_CHUNK
TQ = 512
TK = 512
VT_BLOCK = 256
CUM_BLOCK = 128
GATE_PARTS = 3
ONES_LANE = GATE_PARTS * N_HEADS
MIB = 2 ** 20


def _params(semantics, vmem_mib):
    return pltpu.CompilerParams(dimension_semantics=semantics,
                                vmem_limit_bytes=vmem_mib * MIB)


def _dot(a, b):
    return jnp.dot(a, b, preferred_element_type=F32)


def _dot_nt(a, b):
    return lax.dot_general(a, b, (((1,), (1,)), ((), ())), preferred_element_type=F32)


def _split_bf16(x, parts):
    pieces = []
    for _ in range(parts - 1):
        p = x.astype(BF16)
        pieces.append(p)
        x = x - p.astype(F32)
    pieces.append(x.astype(BF16))
    return pieces


def _rms(h, g):
    ms = jnp.mean(h * h, axis=-1, keepdims=True)
    return h * lax.rsqrt(ms + NORM_EPS) * g


def _rmsnorm_kernel(x_ref, g_ref, o_ref):
    o_ref[...] = _rms(x_ref[...], g_ref[...]).astype(o_ref.dtype)


def _rmsnorm(x, g, out_dtype):
    m, d = x.shape
    return pl.pallas_call(
        _rmsnorm_kernel,
        grid=(m // ROW_TILE,),
        in_specs=[pl.BlockSpec((ROW_TILE, d), lambda i: (i, 0)),
                  pl.BlockSpec((1, d), lambda i: (0, 0))],
        out_specs=pl.BlockSpec((ROW_TILE, d), lambda i: (i, 0)),
        out_shape=jax.ShapeDtypeStruct((m, d), out_dtype),
        compiler_params=_params(("parallel",), 32),
        name="rmsnorm",
    )(x, g.reshape(1, d))


def _proj_kernel(x_ref, w_ref, cs_ref, o_ref):
    x = x_ref[...]
    n = w_ref.shape[1]
    for c in range(n // PROJ_CHUNK):
        sl = slice(c * PROJ_CHUNK, (c + 1) * PROJ_CHUNK)
        o_ref[:, sl] = (_dot(x, w_ref[:, sl]) * cs_ref[:, sl]).astype(o_ref.dtype)


def _proj_gate_kernel(x_ref, w_ref, cs_ref, wf_ref, o_ref, f_ref):
    _proj_kernel(x_ref, w_ref, cs_ref, o_ref)
    f_ref[...] = _dot(x_ref[...], wf_ref[...])


def _proj(xn, w, colscale, w_gate=None):
    m, k = xn.shape
    n = w.shape[1]
    in_specs = [pl.BlockSpec((ROW_TILE, k), lambda i: (i, 0)),
                pl.BlockSpec((k, n), lambda i: (0, 0)),
                pl.BlockSpec((1, n), lambda i: (0, 0))]
    out_specs = [pl.BlockSpec((ROW_TILE, n), lambda i: (i, 0))]
    out_shape = [jax.ShapeDtypeStruct((m, n), BF16)]
    args = [xn, w, colscale]
    body = _proj_kernel
    if w_gate is not None:
        in_specs.append(pl.BlockSpec((k, LANES), lambda i: (0, 0)))
        out_specs.append(pl.BlockSpec((ROW_TILE, LANES), lambda i: (i, 0)))
        out_shape.append(jax.ShapeDtypeStruct((m, LANES), F32))
        args.append(w_gate)
        body = _proj_gate_kernel
    return pl.pallas_call(
        body,
        grid=(m // ROW_TILE,),
        in_specs=in_specs,
        out_specs=out_specs,
        out_shape=out_shape,
        compiler_params=_params(("parallel",), 48),
        name="qkv_proj",
    )(*args)


def _softplus2(y):
    neg_abs = pltpu.bitcast(pltpu.bitcast(y, jnp.uint32) | jnp.uint32(0x80000000), F32)
    return jnp.maximum(y, 0.0) + jnp.log(1.0 + jnp.exp2(neg_abs)) * LOG2E


def _gate_cumsum_kernel(f_ref, b_ref, tri_ref, place_ref, o_ref):
    n_blocks = f_ref.shape[1] // CUM_BLOCK
    tri = tri_ref[...]
    lane = lax.broadcasted_iota(jnp.int32, (CUM_BLOCK, LANES), 1)
    ones = jnp.where((lane >= ONES_LANE) & (lane < ONES_LANE + GATE_PARTS), 1.0, 0.0)

    def body(r, carry):
        rows = pl.ds(pl.multiple_of(r * CUM_BLOCK, CUM_BLOCK), CUM_BLOCK)
        logit2 = (f_ref[0, rows, :] + b_ref[...]) * LOG2E
        ls = -_softplus2(-logit2)
        c2 = carry
        for piece in _split_bf16(ls, GATE_PARTS):
            c2 = c2 + _dot(tri, piece)
        operand = ones
        for i, piece in enumerate(_split_bf16(c2, GATE_PARTS)):
            operand = operand + _dot(piece, place_ref[i])
        o_ref[0, rows, :] = operand.astype(o_ref.dtype)
        return c2[CUM_BLOCK - 1:CUM_BLOCK, :]

    lax.fori_loop(0, n_blocks, body, jnp.zeros((1, LANES), F32))


def _gate_operand(f_logit, b_gate):
    b, s, _ = f_logit.shape
    idx = jnp.arange(CUM_BLOCK)
    tri = (idx[None, :] <= idx[:, None]).astype(BF16)
    lane = jnp.arange(LANES)
    place = jnp.stack([((lane[:, None] < N_HEADS) & (lane[None, :] == i * N_HEADS + lane[:, None]))
                       for i in range(GATE_PARTS)]).astype(BF16)
    return pl.pallas_call(
        _gate_cumsum_kernel,
        grid=(b,),
        in_specs=[pl.BlockSpec((1, s, LANES), lambda i: (i, 0, 0)),
                  pl.BlockSpec((1, LANES), lambda i: (0, 0)),
                  pl.BlockSpec((CUM_BLOCK, CUM_BLOCK), lambda i: (0, 0)),
                  pl.BlockSpec((GATE_PARTS, LANES, LANES), lambda i: (0, 0, 0))],
        out_specs=pl.BlockSpec((1, s, LANES), lambda i: (i, 0, 0)),
        out_shape=jax.ShapeDtypeStruct((b, s, LANES), BF16),
        compiler_params=_params(("parallel",), 32),
        name="gate_cumsum",
    )(f_logit, b_gate, tri, place)


def _stack_heads(q):
    lane = lax.broadcasted_iota(jnp.int32, q.shape, 1)
    zero = jnp.zeros_like(q)
    return jnp.concatenate([jnp.where(lane < HEAD_DIM, q, zero),
                            jnp.where(lane >= HEAD_DIM, q, zero)], axis=0)


def _causal(strict):
    shape = (TK, HEADS_PER_PAIR * TQ)
    key = lax.broadcasted_iota(jnp.int32, shape, 0)
    query = lax.broadcasted_iota(jnp.int32, shape, 1) & (TQ - 1)
    return key < query if strict else key <= query


def _transpose_values(v_ref, store):
    row = lax.broadcasted_iota(jnp.int32, (LANES, LANES), 0)
    col = lax.broadcasted_iota(jnp.int32, (LANES, LANES), 1)
    eye = (row == col).astype(BF16)

    def body(j, carry):
        rows = pl.ds(pl.multiple_of(j * VT_BLOCK, VT_BLOCK), VT_BLOCK)
        store(j, _dot_nt(eye, v_ref[0, rows, :]).astype(BF16))
        return carry

    lax.fori_loop(0, v_ref.shape[1] // VT_BLOCK, body, 0)


def _key_rows(j):
    return pl.ds(pl.multiple_of(j * TK, TK), TK)


def _value_blocks(vt_ref, j, *lead):
    per_block = TK // VT_BLOCK
    return jnp.concatenate([vt_ref[(*lead, j * per_block + i)] for i in range(per_block)], axis=1)


def _sweep(qi, first_stage, second_stage, reverse):
    def pair(t, nxt, masked):
        for parity in range(2):
            @pl.when((t & 1) == parity)
            def _():
                first_stage(nxt, 1 - parity, masked)
                second_stage(qi - t if reverse else t, parity)

    if reverse:
        first_stage(qi, 0, True)
        lax.fori_loop(0, qi, lambda t, c: (pair(t, qi - 1 - t, False), c)[1], 0)
        second_stage(0, qi & 1)
    else:
        pl.when(qi == 0)(lambda: first_stage(0, 0, True))
        pl.when(qi > 0)(lambda: first_stage(0, 0, False))
        lax.fori_loop(0, jnp.maximum(qi - 1, 0), lambda t, c: (pair(t, t + 1, False), c)[1], 0)
        pl.when(qi > 0)(lambda: pair(qi - 1, qi, True))
        second_stage(qi, qi & 1)


def _store_output(o_ref, acc_t):
    out_t = jnp.concatenate([acc_t[:HEAD_DIM, :TQ], acc_t[HEAD_DIM:, TQ:]], axis=0)
    o_ref[0] = out_t.T.astype(o_ref.dtype)


def _fox_kernel(q_ref, k_ref, v_ref, gate_ref, perm_ref, o_ref,
                vt_ref, qa_ref, m_ref, acc_ref, s_buf, m_buf, a_buf):
    pair = pl.program_id(1)
    qi = pl.program_id(2)

    @pl.when(qi == 0)
    def _():
        row = lax.broadcasted_iota(jnp.int32, (LANES, VT_BLOCK), 0)

        def store(j, vt):
            one = jnp.ones_like(vt)
            vt_ref[0, j] = jnp.where(row < HEAD_DIM, vt, one)
            vt_ref[1, j] = jnp.where(row >= HEAD_DIM, vt, one)

        _transpose_values(v_ref, store)

    gate_q = gate_ref[0, pl.ds(pl.multiple_of(qi * TQ, TQ), TQ), :]
    lane = lax.broadcasted_iota(jnp.int32, (TQ, LANES), 1)
    q = q_ref[0]
    zero = jnp.zeros_like(q)
    for hh in range(HEADS_PER_PAIR):
        h = HEADS_PER_PAIR * pair + hh
        minus = jnp.zeros((TQ, LANES), F32)
        for i in range(GATE_PARTS):
            minus = jnp.where(lane == i * N_HEADS + h, -1.0, minus)
        gate_lanes = (_dot(gate_q, perm_ref[h]) + minus).astype(BF16)
        feat = (lane < HEAD_DIM) if hh == 0 else (lane >= HEAD_DIM)
        qa_ref[hh * TQ:(hh + 1) * TQ, :] = jnp.concatenate(
            [jnp.where(feat, q, zero), gate_lanes], axis=1)
    m_ref[...] = jnp.full(m_ref.shape, -jnp.inf, F32)
    acc_ref[...] = jnp.zeros(acc_ref.shape, F32)

    def first_stage(j, slot, masked):
        rows = _key_rows(j)
        k_aug = jnp.concatenate([k_ref[0, rows, :], gate_ref[0, rows, :]], axis=1)
        s = _dot_nt(k_aug, qa_ref[...])
        if masked:
            s = jnp.where(_causal(strict=False), s, -jnp.inf)
        m_old = m_ref[...]
        m_new = jnp.maximum(m_old, jnp.max(s, axis=0, keepdims=True))
        m_ref[...] = m_new
        m_buf[slot] = m_new
        a_buf[slot] = jnp.exp2(m_old - m_new)
        s_buf[slot] = s

    def second_stage(j, slot):
        prob = jnp.exp2(s_buf[slot] - m_buf[slot]).astype(BF16)
        pv = jnp.concatenate(
            [_dot(_value_blocks(vt_ref, j, hh), prob[:, hh * TQ:(hh + 1) * TQ])
             for hh in range(HEADS_PER_PAIR)], axis=1)
        acc_ref[...] = acc_ref[...] * a_buf[slot] + pv

    _sweep(qi, first_stage, second_stage, reverse=False)

    acc = acc_ref[...]
    denom = jnp.concatenate(
        [jnp.broadcast_to(acc[HEAD_DIM:HEAD_DIM + 1, :TQ], (LANES, TQ)),
         jnp.broadcast_to(acc[0:1, TQ:], (LANES, TQ))], axis=1)
    _store_output(o_ref, acc / denom)


def _sb_kernel(q_ref, k_ref, v_ref, tt_ref, o_ref,
               vt_ref, qs_ref, r_ref, acc_ref, z_buf, op_buf):
    qi = pl.program_id(2)

    @pl.when(qi == 0)
    def _():
        def store(j, vt):
            vt_ref[j] = vt

        _transpose_values(v_ref, store)

    qs_ref[...] = _stack_heads(q_ref[0])
    r_ref[...] = jnp.zeros(r_ref.shape, F32)
    acc_ref[...] = jnp.zeros(acc_ref.shape, F32)

    n_sub = TK // CUM_BLOCK

    def first_stage(j, slot, masked):
        z = _dot_nt(k_ref[0, _key_rows(j), :], qs_ref[...])
        if masked:
            z = jnp.where(_causal(strict=True), z, -jnp.inf)
        sp = _softplus2(z)
        z_buf[slot] = z
        for g in range(n_sub):
            op_buf[slot, g] = jnp.concatenate(
                _split_bf16(sp[g * CUM_BLOCK:(g + 1) * CUM_BLOCK], 2), axis=0)

    def second_stage(j, slot):
        r = r_ref[...]
        weights = [None] * n_sub
        for g in reversed(range(n_sub)):
            sums = _dot(tt_ref[...], op_buf[slot, g])
            z = z_buf[slot, g * CUM_BLOCK:(g + 1) * CUM_BLOCK, :]
            weights[g] = jnp.exp2(z - sums[:CUM_BLOCK] - r).astype(BF16)
            r = r + sums[CUM_BLOCK:CUM_BLOCK + 1]
        r_ref[...] = r
        acc_ref[...] += _dot(_value_blocks(vt_ref, j), jnp.concatenate(weights, axis=0))

    _sweep(qi, first_stage, second_stage, reverse=True)
    _store_output(o_ref, acc_ref[...])


def _attention_call(body, name, qkv, extra_inputs, extra_specs, scratch_shapes):
    b, s, _ = qkv.shape
    return pl.pallas_call(
        body,
        grid=(b, N_HEAD_PAIRS, s // TQ),
        in_specs=[pl.BlockSpec((1, TQ, LANES), lambda b, p, i: (b, i, p)),
                  pl.BlockSpec((1, s, LANES), lambda b, p, i: (b, 0, N_HEAD_PAIRS + p)),
                  pl.BlockSpec((1, s, LANES), lambda b, p, i: (b, 0, 2 * N_HEAD_PAIRS + p))
                  ] + extra_specs,
        out_specs=pl.BlockSpec((1, TQ, LANES), lambda b, p, i: (b, i, p)),
        out_shape=jax.ShapeDtypeStruct((b, s, D_MODEL), BF16),
        scratch_shapes=scratch_shapes,
        compiler_params=_params(("parallel", "parallel", "arbitrary"), 48),
        name=name,
    )(qkv, qkv, qkv, *extra_inputs)


def _fox_attention(qkv, gate):
    b, s, _ = qkv.shape
    lane = jnp.arange(LANES)
    perm = jnp.stack([
        sum(((lane[:, None] == i * N_HEADS + h) & (lane[None, :] == ONES_LANE + i))
            for i in range(GATE_PARTS))
        for h in range(N_HEADS)]).astype(BF16)
    return _attention_call(
        _fox_kernel, "fox_attention", qkv, [gate, perm],
        [pl.BlockSpec((1, s, LANES), lambda b, p, i: (b, 0, 0)),
         pl.BlockSpec((N_HEADS, LANES, LANES), lambda b, p, i: (0, 0, 0))],
        [pltpu.VMEM((HEADS_PER_PAIR, s // VT_BLOCK, LANES, VT_BLOCK), BF16),
         pltpu.VMEM((HEADS_PER_PAIR * TQ, 2 * LANES), BF16),
         pltpu.VMEM((1, HEADS_PER_PAIR * TQ), F32),
         pltpu.VMEM((LANES, HEADS_PER_PAIR * TQ), F32),
         pltpu.VMEM((2, TK, HEADS_PER_PAIR * TQ), F32),
         pltpu.VMEM((2, 1, HEADS_PER_PAIR * TQ), F32),
         pltpu.VMEM((2, 1, HEADS_PER_PAIR * TQ), F32)])


def _sb_attention(qkv):
    b, s, _ = qkv.shape
    idx = jnp.arange(CUM_BLOCK)
    suffix = (idx[None, :] >= idx[:, None]).astype(BF16)
    tt = jnp.concatenate([jnp.concatenate([suffix, suffix], axis=1),
                          jnp.ones((SUBLANES, 2 * CUM_BLOCK), BF16)], axis=0)
    return _attention_call(
        _sb_kernel, "sb_attention", qkv, [tt],
        [pl.BlockSpec((CUM_BLOCK + SUBLANES, 2 * CUM_BLOCK), lambda b, p, i: (0, 0))],
        [pltpu.VMEM((s // VT_BLOCK, LANES, VT_BLOCK), BF16),
         pltpu.VMEM((HEADS_PER_PAIR * TQ, LANES), BF16),
         pltpu.VMEM((1, HEADS_PER_PAIR * TQ), F32),
         pltpu.VMEM((LANES, HEADS_PER_PAIR * TQ), F32),
         pltpu.VMEM((2, TK, HEADS_PER_PAIR * TQ), F32),
         pltpu.VMEM((2, TK // CUM_BLOCK, 2 * CUM_BLOCK, HEADS_PER_PAIR * TQ), BF16)])


def _mm_resid_norm_kernel(a_ref, w_ref, h_ref, g_ref, hs_ref, xn_ref):
    h = _dot(a_ref[...], w_ref[...]) + h_ref[...]
    hs_ref[...] = h
    xn_ref[...] = _rms(h, g_ref[...]).astype(xn_ref.dtype)


def _mm_resid_final_kernel(a_ref, w_ref, h_ref, g_ref, o_ref):
    h = _dot(a_ref[...], w_ref[...]) + h_ref[...]
    o_ref[...] = _rms(h, g_ref[...]).astype(o_ref.dtype)


def _mm_resid_norm(a, w, hs, g, final=False):
    m, k = a.shape
    d = w.shape[1]
    row = lambda i: (i, 0)
    in_specs = [pl.BlockSpec((ROW_TILE, k), row),
                pl.BlockSpec((k, d), lambda i: (0, 0)),
                pl.BlockSpec((ROW_TILE, d), row),
                pl.BlockSpec((1, d), lambda i: (0, 0))]
    if final:
        body = _mm_resid_final_kernel
        out_specs = pl.BlockSpec((ROW_TILE, d), row)
        out_shape = jax.ShapeDtypeStruct((m, d), F32)
    else:
        body = _mm_resid_norm_kernel
        out_specs = [pl.BlockSpec((ROW_TILE, d), row), pl.BlockSpec((ROW_TILE, d), row)]
        out_shape = [jax.ShapeDtypeStruct((m, d), F32), jax.ShapeDtypeStruct((m, d), BF16)]
    return pl.pallas_call(
        body,
        grid=(m // ROW_TILE,),
        in_specs=in_specs,
        out_specs=out_specs,
        out_shape=out_shape,
        compiler_params=_params(("parallel",), 48),
        name="matmul_residual_norm",
    )(a, w, hs, g.reshape(1, d))


def _ffn_up_kernel(x_ref, w_ref, wc_ref, bc_ref, o_ref, carry_ref, *, tiles_per_seq):
    i = pl.program_id(0)

    @pl.when(i % tiles_per_seq == 0)
    def _():
        carry_ref[...] = jnp.zeros(carry_ref.shape, F32)

    x = x_ref[...]
    tm = x.shape[0]
    row = lax.broadcasted_iota(jnp.int32, (tm, 2 * FF_CHUNK), 0)
    for c in range(N_FF_CHUNKS):
        h = _dot(x, w_ref[c])
        prev = carry_ref[c]
        p1 = prev[7:8]
        p2 = prev[6:7]
        h1 = jnp.where(row == 0, p1, pltpu.roll(h, 1, 0))
        h2 = jnp.where(row == 0, p2, jnp.where(row == 1, p1, pltpu.roll(h, 2, 0)))
        wc = wc_ref[c]
        hc = bc_ref[c] + wc[0:1] * h2 + wc[1:2] * h1 + wc[2:3] * h
        u = hc[:, :FF_CHUNK]
        g = hc[:, FF_CHUNK:]
        act = g * u / (1.0 + jnp.exp(-g))
        o_ref[:, c * FF_CHUNK:(c + 1) * FF_CHUNK] = act.astype(o_ref.dtype)
        carry_ref[c] = h[tm - 8:tm]


def _ffn_up(xn, w_chunks, wc_chunks, bc_chunks, seq_len):
    m, k = xn.shape
    kern = functools.partial(_ffn_up_kernel, tiles_per_seq=seq_len // ROW_TILE)
    return pl.pallas_call(
        kern,
        grid=(m // ROW_TILE,),
        in_specs=[pl.BlockSpec((ROW_TILE, k), lambda i: (i, 0)),
                  pl.BlockSpec((N_FF_CHUNKS, k, 2 * FF_CHUNK), lambda i: (0, 0, 0)),
                  pl.BlockSpec((N_FF_CHUNKS, CONV_WIDTH, 2 * FF_CHUNK), lambda i: (0, 0, 0)),
                  pl.BlockSpec((N_FF_CHUNKS, 1, 2 * FF_CHUNK), lambda i: (0, 0, 0))],
        out_specs=pl.BlockSpec((ROW_TILE, D_FF), lambda i: (i, 0)),
        out_shape=jax.ShapeDtypeStruct((m, D_FF), BF16),
        scratch_shapes=[pltpu.VMEM((N_FF_CHUNKS, 8, 2 * FF_CHUNK), F32)],
        compiler_params=_params(("arbitrary",), 56),
        name="ffn_up_conv_gate",
    )(xn, w_chunks, wc_chunks, bc_chunks)


def _chunk_columns(t):
    r = t.shape[0]
    t = t.reshape(r, 2, N_FF_CHUNKS, FF_CHUNK)
    return t.transpose(2, 0, 1, 3).reshape(N_FF_CHUNKS, r, 2 * FF_CHUNK)


def kernel(x, attn_norm, ffn_norm, final_norm, fox_w_qkvf, fox_b_f, fox_w_o,
           sb_w_qkv, sb_w_o, ffn_w_up, ffn_w_conv, ffn_b_conv, ffn_w_down):
    b, s, d = x.shape
    m = b * s
    assert d == D_MODEL and s % TQ == 0 and s % ROW_TILE == 0

    colscale = jnp.concatenate([jnp.full((1, d), ATTN_SCALE * LOG2E, F32),
                                jnp.ones((1, 2 * d), F32)], axis=1)

    hs = x.reshape(m, d)
    xn = _rmsnorm(hs, attn_norm[0], BF16)
    out = None
    for i in range(DEPTH):
        j = i // 2
        if i % 2 == 0:
            w = fox_w_qkvf[j]
            w_gate = jnp.pad(w[:, 3 * d:], ((0, 0), (0, LANES - N_HEADS))).astype(BF16)
            qkv, f_logit = _proj(xn, w[:, :3 * d].astype(BF16), colscale, w_gate)
            b_gate = jnp.pad(fox_b_f[j], (0, LANES - N_HEADS)).reshape(1, LANES)
            gate = _gate_operand(f_logit.reshape(b, s, LANES), b_gate)
            o = _fox_attention(qkv.reshape(b, s, 3 * d), gate)
            w_o = fox_w_o[j]
        else:
            qkv = _proj(xn, sb_w_qkv[j].astype(BF16), colscale)[0]
            o = _sb_attention(qkv.reshape(b, s, 3 * d))
            w_o = sb_w_o[j]
        hs, xn = _mm_resid_norm(o.reshape(m, d), w_o.astype(BF16), hs, ffn_norm[i])
        act = _ffn_up(xn,
                      _chunk_columns(ffn_w_up[i]).astype(BF16),
                      _chunk_columns(ffn_w_conv[i]),
                      _chunk_columns(ffn_b_conv[i].reshape(1, -1)),
                      s)
        w_down = ffn_w_down[i].astype(BF16)
        if i + 1 < DEPTH:
            hs, xn = _mm_resid_norm(act, w_down, hs, attn_norm[i + 1])
        else:
            out = _mm_resid_norm(act, w_down, hs, final_norm, final=True)
    return out.reshape(b, s, d)
```

```python
import functools
import math

import jax
import jax.numpy as jnp
from jax import lax
from jax.experimental import pallas as pl
from jax.experimental.pallas import tpu as pltpu

D_MODEL = 1024
N_HEADS = 16
HEAD_DIM = D_MODEL // N_HEADS
D_FF = 2816
CONV_WIDTH = 3
DEPTH = 4
NORM_EPS = 1e-6
ATTN_SCALE = HEAD_DIM ** -0.5
LOG2E = math.log2(math.e)

F32 = jnp.float32
BF16 = jnp.bfloat16

LANES = 128
SUBLANES = 8
HEADS_PER_PAIR = LANES // HEAD_DIM
N_HEAD_PAIRS = N_HEADS // HEADS_PER_PAIR
ROW_TILE = 512
PROJ_CHUNK = 512
FF_CHUNK = 256
N_FF_CHUNKS = D_FF // FF_CHUNK
TQ = 512
TK = 512
VT_BLOCK = 256
CUM_BLOCK = 128
SB_BLOCK = 256
GATE_PARTS = 3
ONES_LANE = GATE_PARTS * N_HEADS
MIB = 2 ** 20


def _params(semantics, vmem_mib):
    return pltpu.CompilerParams(dimension_semantics=semantics,
                                vmem_limit_bytes=vmem_mib * MIB)


def _dot(a, b):
    return jnp.dot(a, b, preferred_element_type=F32)


def _dot_nt(a, b):
    return lax.dot_general(a, b, (((1,), (1,)), ((), ())), preferred_element_type=F32)


def _split_bf16(x, parts):
    pieces = []
    for _ in range(parts - 1):
        p = x.astype(BF16)
        pieces.append(p)
        x = x - p.astype(F32)
    pieces.append(x.astype(BF16))
    return pieces


def _rms(h, g):
    ms = jnp.mean(h * h, axis=-1, keepdims=True)
    return h * lax.rsqrt(ms + NORM_EPS) * g


def _rmsnorm_kernel(x_ref, g_ref, o_ref):
    o_ref[...] = _rms(x_ref[...], g_ref[...]).astype(o_ref.dtype)


def _rmsnorm(x, g, out_dtype):
    m, d = x.shape
    return pl.pallas_call(
        _rmsnorm_kernel,
        grid=(m // ROW_TILE,),
        in_specs=[pl.BlockSpec((ROW_TILE, d), lambda i: (i, 0)),
                  pl.BlockSpec((1, d), lambda i: (0, 0))],
        out_specs=pl.BlockSpec((ROW_TILE, d), lambda i: (i, 0)),
        out_shape=jax.ShapeDtypeStruct((m, d), out_dtype),
        compiler_params=_params(("parallel",), 32),
        name="rmsnorm",
    )(x, g.reshape(1, d))


def _proj_kernel(x_ref, w_ref, cs_ref, o_ref):
    x = x_ref[...]
    n = w_ref.shape[1]
    for c in range(n // PROJ_CHUNK):
        sl = slice(c * PROJ_CHUNK, (c + 1) * PROJ_CHUNK)
        o_ref[:, sl] = (_dot(x, w_ref[:, sl]) * cs_ref[:, sl]).astype(o_ref.dtype)


def _proj_gate_kernel(x_ref, w_ref, cs_ref, wf_ref, o_ref, f_ref):
    _proj_kernel(x_ref, w_ref, cs_ref, o_ref)
    f_ref[...] = _dot(x_ref[...], wf_ref[...])


def _proj(xn, w, colscale, w_gate=None):
    m, k = xn.shape
    n = w.shape[1]
    in_specs = [pl.BlockSpec((ROW_TILE, k), lambda i: (i, 0)),
                pl.BlockSpec((k, n), lambda i: (0, 0)),
                pl.BlockSpec((1, n), lambda i: (0, 0))]
    out_specs = [pl.BlockSpec((ROW_TILE, n), lambda i: (i, 0))]
    out_shape = [jax.ShapeDtypeStruct((m, n), BF16)]
    args = [xn, w, colscale]
    body = _proj_kernel
    if w_gate is not None:
        in_specs.append(pl.BlockSpec((k, LANES), lambda i: (0, 0)))
        out_specs.append(pl.BlockSpec((ROW_TILE, LANES), lambda i: (i, 0)))
        out_shape.append(jax.ShapeDtypeStruct((m, LANES), F32))
        args.append(w_gate)
        body = _proj_gate_kernel
    return pl.pallas_call(
        body,
        grid=(m // ROW_TILE,),
        in_specs=in_specs,
        out_specs=out_specs,
        out_shape=out_shape,
        compiler_params=_params(("parallel",), 48),
        name="qkv_proj",
    )(*args)


def _softplus2(y):
    neg_abs = pltpu.bitcast(pltpu.bitcast(y, jnp.uint32) | jnp.uint32(0x80000000), F32)
    return jnp.maximum(y, 0.0) + jnp.log(1.0 + jnp.exp2(neg_abs)) * LOG2E


def _gate_cumsum_kernel(f_ref, b_ref, tri_ref, place_ref, o_ref):
    n_blocks = f_ref.shape[1] // CUM_BLOCK
    tri = tri_ref[...]
    lane = lax.broadcasted_iota(jnp.int32, (CUM_BLOCK, LANES), 1)
    ones = jnp.where((lane >= ONES_LANE) & (lane < ONES_LANE + GATE_PARTS), 1.0, 0.0)

    def body(r, carry):
        rows = pl.ds(pl.multiple_of(r * CUM_BLOCK, CUM_BLOCK), CUM_BLOCK)
        logit2 = (f_ref[0, rows, :] + b_ref[...]) * LOG2E
        ls = -_softplus2(-logit2)
        c2 = carry
        for piece in _split_bf16(ls, GATE_PARTS):
            c2 = c2 + _dot(tri, piece)
        operand = ones
        for i, piece in enumerate(_split_bf16(c2, GATE_PARTS)):
            operand = operand + _dot(piece, place_ref[i])
        o_ref[0, rows, :] = operand.astype(o_ref.dtype)
        return c2[CUM_BLOCK - 1:CUM_BLOCK, :]

    lax.fori_loop(0, n_blocks, body, jnp.zeros((1, LANES), F32))


def _gate_operand(f_logit, b_gate):
    b, s, _ = f_logit.shape
    idx = jnp.arange(CUM_BLOCK)
    tri = (idx[None, :] <= idx[:, None]).astype(BF16)
    lane = jnp.arange(LANES)
    place = jnp.stack([((lane[:, None] < N_HEADS) & (lane[None, :] == i * N_HEADS + lane[:, None]))
                       for i in range(GATE_PARTS)]).astype(BF16)
    return pl.pallas_call(
        _gate_cumsum_kernel,
        grid=(b,),
        in_specs=[pl.BlockSpec((1, s, LANES), lambda i: (i, 0, 0)),
                  pl.BlockSpec((1, LANES), lambda i: (0, 0)),
                  pl.BlockSpec((CUM_BLOCK, CUM_BLOCK), lambda i: (0, 0)),
                  pl.BlockSpec((GATE_PARTS, LANES, LANES), lambda i: (0, 0, 0))],
        out_specs=pl.BlockSpec((1, s, LANES), lambda i: (i, 0, 0)),
        out_shape=jax.ShapeDtypeStruct((b, s, LANES), BF16),
        compiler_params=_params(("parallel",), 32),
        name="gate_cumsum",
    )(f_logit, b_gate, tri, place)


def _stack_heads(q):
    lane = lax.broadcasted_iota(jnp.int32, q.shape, 1)
    zero = jnp.zeros_like(q)
    return jnp.concatenate([jnp.where(lane < HEAD_DIM, q, zero),
                            jnp.where(lane >= HEAD_DIM, q, zero)], axis=0)


def _causal(strict):
    shape = (TK, HEADS_PER_PAIR * TQ)
    key = lax.broadcasted_iota(jnp.int32, shape, 0)
    query = lax.broadcasted_iota(jnp.int32, shape, 1) & (TQ - 1)
    return key < query if strict else key <= query


def _transpose_values(v_ref, store):
    row = lax.broadcasted_iota(jnp.int32, (LANES, LANES), 0)
    col = lax.broadcasted_iota(jnp.int32, (LANES, LANES), 1)
    eye = (row == col).astype(BF16)

    def body(j, carry):
        rows = pl.ds(pl.multiple_of(j * VT_BLOCK, VT_BLOCK), VT_BLOCK)
        store(j, _dot_nt(eye, v_ref[0, rows, :]).astype(BF16))
        return carry

    lax.fori_loop(0, v_ref.shape[1] // VT_BLOCK, body, 0)


def _key_rows(j):
    return pl.ds(pl.multiple_of(j * TK, TK), TK)


def _value_blocks(vt_ref, j, *lead):
    per_block = TK // VT_BLOCK
    return jnp.concatenate([vt_ref[(*lead, j * per_block + i)] for i in range(per_block)], axis=1)


def _sweep_ascending(qi, first_stage, second_stage):
    def pair(t, masked):
        for parity in range(2):
            @pl.when((t & 1) == parity)
            def _():
                first_stage(t + 1, 1 - parity, masked)
                second_stage(t, parity)

    pl.when(qi == 0)(lambda: first_stage(0, 0, True))
    pl.when(qi > 0)(lambda: first_stage(0, 0, False))
    lax.fori_loop(0, jnp.maximum(qi - 1, 0), lambda t, c: (pair(t, False), c)[1], 0)
    pl.when(qi > 0)(lambda: pair(qi - 1, True))
    second_stage(qi, qi & 1)


def _sweep_descending(qi, logits, weights, accumulate):
    def by_parity(u, fn):
        for parity in range(2):
            pl.when((u & 1) == parity)(functools.partial(fn, parity))

    def steady(u, parity):
        logits(qi - u, parity, False)
        weights(1 - parity)
        accumulate(qi - (u - 2), parity)

    def drain(parity):
        weights(parity)
        accumulate(1, 1 - parity)

    logits(qi, 0, True)

    @pl.when(qi == 0)
    def _():
        weights(0)

    @pl.when(qi > 0)
    def _():
        logits(qi - 1, 1, False)
        weights(0)

    lax.fori_loop(2, qi + 1, lambda u, c: (by_parity(u, functools.partial(steady, u)), c)[1], 0)
    pl.when(qi > 0)(lambda: by_parity(qi, drain))
    accumulate(0, qi & 1)


def _store_output(o_ref, acc_t):
    out_t = jnp.concatenate([acc_t[:HEAD_DIM, :TQ], acc_t[HEAD_DIM:, TQ:]], axis=0)
    o_ref[0] = out_t.T.astype(o_ref.dtype)


def _fox_kernel(q_ref, k_ref, v_ref, gate_ref, perm_ref, o_ref,
                vt_ref, qa_ref, m_ref, acc_ref, s_buf, m_buf, a_buf):
    pair = pl.program_id(1)
    qi = pl.program_id(2)

    @pl.when(qi == 0)
    def _():
        row = lax.broadcasted_iota(jnp.int32, (LANES, VT_BLOCK), 0)

        def store(j, vt):
            one = jnp.ones_like(vt)
            vt_ref[0, j] = jnp.where(row < HEAD_DIM, vt, one)
            vt_ref[1, j] = jnp.where(row >= HEAD_DIM, vt, one)

        _transpose_values(v_ref, store)

    gate_q = gate_ref[0, pl.ds(pl.multiple_of(qi * TQ, TQ), TQ), :]
    lane = lax.broadcasted_iota(jnp.int32, (TQ, LANES), 1)
    q = q_ref[0]
    zero = jnp.zeros_like(q)
    for hh in range(HEADS_PER_PAIR):
        h = HEADS_PER_PAIR * pair + hh
        minus = jnp.zeros((TQ, LANES), F32)
        for i in range(GATE_PARTS):
            minus = jnp.where(lane == i * N_HEADS + h, -1.0, minus)
        gate_lanes = (_dot(gate_q, perm_ref[h]) + minus).astype(BF16)
        feat = (lane < HEAD_DIM) if hh == 0 else (lane >= HEAD_DIM)
        qa_ref[hh * TQ:(hh + 1) * TQ, :] = jnp.concatenate(
            [jnp.where(feat, q, zero), gate_lanes], axis=1)
    m_ref[...] = jnp.full(m_ref.shape, -jnp.inf, F32)
    acc_ref[...] = jnp.zeros(acc_ref.shape, F32)

    def first_stage(j, slot, masked):
        rows = _key_rows(j)
        k_aug = jnp.concatenate([k_ref[0, rows, :], gate_ref[0, rows, :]], axis=1)
        s = _dot_nt(k_aug, qa_ref[...])
        if masked:
            s = jnp.where(_causal(strict=False), s, -jnp.inf)
        m_old = m_ref[...]
        m_new = jnp.maximum(m_old, jnp.max(s, axis=0, keepdims=True))
        m_ref[...] = m_new
        m_buf[slot] = m_new
        a_buf[slot] = jnp.exp2(m_old - m_new)
        s_buf[slot] = s

    def second_stage(j, slot):
        prob = jnp.exp2(s_buf[slot] - m_buf[slot]).astype(BF16)
        pv = jnp.concatenate(
            [_dot(_value_blocks(vt_ref, j, hh), prob[:, hh * TQ:(hh + 1) * TQ])
             for hh in range(HEADS_PER_PAIR)], axis=1)
        acc_ref[...] = acc_ref[...] * a_buf[slot] + pv

    _sweep_ascending(qi, first_stage, second_stage)

    acc = acc_ref[...]
    denom = jnp.concatenate(
        [jnp.broadcast_to(acc[HEAD_DIM:HEAD_DIM + 1, :TQ], (LANES, TQ)),
         jnp.broadcast_to(acc[0:1, TQ:], (LANES, TQ))], axis=1)
    _store_output(o_ref, acc / denom)


def _sb_kernel(q_ref, k_ref, v_ref, tt_ref, o_ref,
               vt_ref, qs_ref, r_ref, acc_ref, z_buf, w_buf):
    qi = pl.program_id(2)

    @pl.when(qi == 0)
    def _():
        def store(j, vt):
            vt_ref[j] = vt

        _transpose_values(v_ref, store)

    qs_ref[...] = _stack_heads(q_ref[0])
    r_ref[...] = jnp.zeros(r_ref.shape, F32)
    acc_ref[...] = jnp.zeros(acc_ref.shape, F32)

    n_sub = TK // SB_BLOCK

    def logits(j, slot, masked):
        z = _dot_nt(k_ref[0, _key_rows(j), :], qs_ref[...])
        if masked:
            z = jnp.where(_causal(strict=True), z, -jnp.inf)
        z_buf[slot] = z

    def weights(slot):
        r = r_ref[...]
        for g in reversed(range(n_sub)):
            rows = slice(g * SB_BLOCK, (g + 1) * SB_BLOCK)
            z = z_buf[slot, rows, :]
            sp = _softplus2(z)
            sums = _dot(tt_ref[...], sp.astype(BF16))
            w_buf[slot, rows, :] = jnp.exp2(z - sp - sums[:SB_BLOCK] - r).astype(BF16)
            r = r + sums[SB_BLOCK:SB_BLOCK + 1]
        r_ref[...] = r

    def accumulate(j, slot):
        acc_ref[...] += _dot(_value_blocks(vt_ref, j), w_buf[slot])

    _sweep_descending(qi, logits, weights, accumulate)
    _store_output(o_ref, acc_ref[...])


def _attention_call(body, name, qkv, extra_inputs, extra_specs, scratch_shapes):
    b, s, _ = qkv.shape
    return pl.pallas_call(
        body,
        grid=(b, N_HEAD_PAIRS, s // TQ),
        in_specs=[pl.BlockSpec((1, TQ, LANES), lambda b, p, i: (b, i, p)),
                  pl.BlockSpec((1, s, LANES), lambda b, p, i: (b, 0, N_HEAD_PAIRS + p)),
                  pl.BlockSpec((1, s, LANES), lambda b, p, i: (b, 0, 2 * N_HEAD_PAIRS + p))
                  ] + extra_specs,
        out_specs=pl.BlockSpec((1, TQ, LANES), lambda b, p, i: (b, i, p)),
        out_shape=jax.ShapeDtypeStruct((b, s, D_MODEL), BF16),
        scratch_shapes=scratch_shapes,
        compiler_params=_params(("parallel", "parallel", "arbitrary"), 48),
        name=name,
    )(qkv, qkv, qkv, *extra_inputs)


def _fox_attention(qkv, gate):
    b, s, _ = qkv.shape
    lane = jnp.arange(LANES)
    perm = jnp.stack([
        sum(((lane[:, None] == i * N_HEADS + h) & (lane[None, :] == ONES_LANE + i))
            for i in range(GATE_PARTS))
        for h in range(N_HEADS)]).astype(BF16)
    return _attention_call(
        _fox_kernel, "fox_attention", qkv, [gate, perm],
        [pl.BlockSpec((1, s, LANES), lambda b, p, i: (b, 0, 0)),
         pl.BlockSpec((N_HEADS, LANES, LANES), lambda b, p, i: (0, 0, 0))],
        [pltpu.VMEM((HEADS_PER_PAIR, s // VT_BLOCK, LANES, VT_BLOCK), BF16),
         pltpu.VMEM((HEADS_PER_PAIR * TQ, 2 * LANES), BF16),
         pltpu.VMEM((1, HEADS_PER_PAIR * TQ), F32),
         pltpu.VMEM((LANES, HEADS_PER_PAIR * TQ), F32),
         pltpu.VMEM((2, TK, HEADS_PER_PAIR * TQ), F32),
         pltpu.VMEM((2, 1, HEADS_PER_PAIR * TQ), F32),
         pltpu.VMEM((2, 1, HEADS_PER_PAIR * TQ), F32)])


def _sb_attention(qkv):
    b, s, _ = qkv.shape
    idx = jnp.arange(SB_BLOCK)
    later = (idx[None, :] > idx[:, None]).astype(BF16)
    tt = jnp.concatenate([later, jnp.ones((SUBLANES, SB_BLOCK), BF16)], axis=0)
    return _attention_call(
        _sb_kernel, "sb_attention", qkv, [tt],
        [pl.BlockSpec((SB_BLOCK + SUBLANES, SB_BLOCK), lambda b, p, i: (0, 0))],
        [pltpu.VMEM((s // VT_BLOCK, LANES, VT_BLOCK), BF16),
         pltpu.VMEM((HEADS_PER_PAIR * TQ, LANES), BF16),
         pltpu.VMEM((1, HEADS_PER_PAIR * TQ), F32),
         pltpu.VMEM((LANES, HEADS_PER_PAIR * TQ), F32),
         pltpu.VMEM((2, TK, HEADS_PER_PAIR * TQ), F32),
         pltpu.VMEM((2, TK, HEADS_PER_PAIR * TQ), BF16)])


def _mm_resid_norm_kernel(a_ref, w_ref, h_ref, g_ref, hs_ref, xn_ref):
    h = _dot(a_ref[...], w_ref[...]) + h_ref[...]
    hs_ref[...] = h
    xn_ref[...] = _rms(h, g_ref[...]).astype(xn_ref.dtype)


def _mm_resid_final_kernel(a_ref, w_ref, h_ref, g_ref, o_ref):
    h = _dot(a_ref[...], w_ref[...]) + h_ref[...]
    o_ref[...] = _rms(h, g_ref[...]).astype(o_ref.dtype)


def _mm_resid_norm(a, w, hs, g, final=False):
    m, k = a.shape
    d = w.shape[1]
    row = lambda i: (i, 0)
    in_specs = [pl.BlockSpec((ROW_TILE, k), row),
                pl.BlockSpec((k, d), lambda i: (0, 0)),
                pl.BlockSpec((ROW_TILE, d), row),
                pl.BlockSpec((1, d), lambda i: (0, 0))]
    if final:
        body = _mm_resid_final_kernel
        out_specs = pl.BlockSpec((ROW_TILE, d), row)
        out_shape = jax.ShapeDtypeStruct((m, d), F32)
    else:
        body = _mm_resid_norm_kernel
        out_specs = [pl.BlockSpec((ROW_TILE, d), row), pl.BlockSpec((ROW_TILE, d), row)]
        out_shape = [jax.ShapeDtypeStruct((m, d), F32), jax.ShapeDtypeStruct((m, d), BF16)]
    return pl.pallas_call(
        body,
        grid=(m // ROW_TILE,),
        in_specs=in_specs,
        out_specs=out_specs,
        out_shape=out_shape,
        compiler_params=_params(("parallel",), 48),
        name="matmul_residual_norm",
    )(a, w, hs, g.reshape(1, d))


def _ffn_up_kernel(x_ref, w_ref, wc_ref, bc_ref, o_ref, carry_ref, *, tiles_per_seq):
    i = pl.program_id(0)

    @pl.when(i % tiles_per_seq == 0)
    def _():
        carry_ref[...] = jnp.zeros(carry_ref.shape, F32)

    x = x_ref[...]
    tm = x.shape[0]
    row = lax.broadcasted_iota(jnp.int32, (SUBLANES, 2 * FF_CHUNK), 0)

    def shifted(h, prev, shift):
        rolled = pltpu.roll(h, shift, 0)
        head = rolled[:SUBLANES]
        for i in range(shift):
            head = jnp.where(row == i, prev[SUBLANES - shift + i:SUBLANES - shift + i + 1], head)
        return jnp.concatenate([head, rolled[SUBLANES:]], axis=0)

    for c in range(N_FF_CHUNKS):
        h = _dot(x, w_ref[c])
        prev = carry_ref[c]
        h1 = shifted(h, prev, 1)
        h2 = shifted(h, prev, 2)
        wc = wc_ref[c]
        hc = bc_ref[c] + wc[0:1] * h2 + wc[1:2] * h1 + wc[2:3] * h
        u = hc[:, :FF_CHUNK]
        g = hc[:, FF_CHUNK:]
        act = g * u / (1.0 + jnp.exp(-g))
        o_ref[:, c * FF_CHUNK:(c + 1) * FF_CHUNK] = act.astype(o_ref.dtype)
        carry_ref[c] = h[tm - SUBLANES:tm]


def _ffn_up(xn, w_chunks, wc_chunks, bc_chunks, seq_len):
    m, k = xn.shape
    kern = functools.partial(_ffn_up_kernel, tiles_per_seq=seq_len // ROW_TILE)
    return pl.pallas_call(
        kern,
        grid=(m // ROW_TILE,),
        in_specs=[pl.BlockSpec((ROW_TILE, k), lambda i: (i, 0)),
                  pl.BlockSpec((N_FF_CHUNKS, k, 2 * FF_CHUNK), lambda i: (0, 0, 0)),
                  pl.BlockSpec((N_FF_CHUNKS, CONV_WIDTH, 2 * FF_CHUNK), lambda i: (0, 0, 0)),
                  pl.BlockSpec((N_FF_CHUNKS, 1, 2 * FF_CHUNK), lambda i: (0, 0, 0))],
        out_specs=pl.BlockSpec((ROW_TILE, D_FF), lambda i: (i, 0)),
        out_shape=jax.ShapeDtypeStruct((m, D_FF), BF16),
        scratch_shapes=[pltpu.VMEM((N_FF_CHUNKS, SUBLANES, 2 * FF_CHUNK), F32)],
        compiler_params=_params(("arbitrary",), 56),
        name="ffn_up_conv_gate",
    )(xn, w_chunks, wc_chunks, bc_chunks)


def _chunk_columns(t):
    r = t.shape[0]
    t = t.reshape(r, 2, N_FF_CHUNKS, FF_CHUNK)
    return t.transpose(2, 0, 1, 3).reshape(N_FF_CHUNKS, r, 2 * FF_CHUNK)


def kernel(x, attn_norm, ffn_norm, final_norm, fox_w_qkvf, fox_b_f, fox_w_o,
           sb_w_qkv, sb_w_o, ffn_w_up, ffn_w_conv, ffn_b_conv, ffn_w_down):
    b, s, d = x.shape
    m = b * s
    assert d == D_MODEL and s % TQ == 0 and s % ROW_TILE == 0

    colscale = jnp.concatenate([jnp.full((1, d), ATTN_SCALE * LOG2E, F32),
                                jnp.ones((1, 2 * d), F32)], axis=1)

    hs = x.reshape(m, d)
    xn = _rmsnorm(hs, attn_norm[0], BF16)
    out = None
    for i in range(DEPTH):
        j = i // 2
        if i % 2 == 0:
            w = fox_w_qkvf[j]
            w_gate = jnp.pad(w[:, 3 * d:], ((0, 0), (0, LANES - N_HEADS))).astype(BF16)
            qkv, f_logit = _proj(xn, w[:, :3 * d].astype(BF16), colscale, w_gate)
            b_gate = jnp.pad(fox_b_f[j], (0, LANES - N_HEADS)).reshape(1, LANES)
            gate = _gate_operand(f_logit.reshape(b, s, LANES), b_gate)
            o = _fox_attention(qkv.reshape(b, s, 3 * d), gate)
            w_o = fox_w_o[j]
        else:
            qkv = _proj(xn, sb_w_qkv[j].astype(BF16), colscale)[0]
            o = _sb_attention(qkv.reshape(b, s, 3 * d))
            w_o = sb_w_o[j]
        hs, xn = _mm_resid_norm(o.reshape(m, d), w_o.astype(BF16), hs, ffn_norm[i])
        act = _ffn_up(xn,
                      _chunk_columns(ffn_w_up[i]).astype(BF16),
                      _chunk_columns(ffn_w_conv[i]),
                      _chunk_columns(ffn_b_conv[i].reshape(1, -1)),
                      s)
        w_down = ffn_w_down[i].astype(BF16)
        if i + 1 < DEPTH:
            hs, xn = _mm_resid_norm(act, w_down, hs, attn_norm[i + 1])
        else:
            out = _mm_resid_norm(act, w_down, hs, final_norm, final=True)
    return out.reshape(b, s, d)
```

```python
import functools
import math

import jax
import jax.numpy as jnp
from jax import lax
from jax.experimental import pallas as pl
from jax.experimental.pallas import tpu as pltpu

D_MODEL = 1024
N_HEADS = 16
HEAD_DIM = D_MODEL // N_HEADS
D_FF = 2816
CONV_WIDTH = 3
DEPTH = 4
NORM_EPS = 1e-6
ATTN_SCALE = HEAD_DIM ** -0.5
LOG2E = math.log2(math.e)

F32 = jnp.float32
BF16 = jnp.bfloat16

LANES = 128
SUBLANES = 8
HEADS_PER_PAIR = LANES // HEAD_DIM
N_HEAD_PAIRS = N_HEADS // HEADS_PER_PAIR
ROW_TILE = 512
PROJ_CHUNK = 512
FF_CHUNK = 256
N_FF_CHUNKS = D_FF // FF_CHUNK
TQ = 512
TK = 512
VT_BLOCK = 256
CUM_BLOCK = 128
SB_BLOCK = 256
GATE_PARTS = 3
ONES_LANE = GATE_PARTS * N_HEADS
MIB = 2 ** 20


def _params(semantics, vmem_mib):
    return pltpu.CompilerParams(dimension_semantics=semantics,
                                vmem_limit_bytes=vmem_mib * MIB)


def _dot(a, b):
    return jnp.dot(a, b, preferred_element_type=F32)


def _dot_nt(a, b):
    return lax.dot_general(a, b, (((1,), (1,)), ((), ())), preferred_element_type=F32)


def _split_bf16(x, parts):
    pieces = []
    for _ in range(parts - 1):
        p = x.astype(BF16)
        pieces.append(p)
        x = x - p.astype(F32)
    pieces.append(x.astype(BF16))
    return pieces


def _rms(h, g):
    ms = jnp.mean(h * h, axis=-1, keepdims=True)
    return h * lax.rsqrt(ms + NORM_EPS) * g


def _rmsnorm_kernel(x_ref, g_ref, o_ref):
    o_ref[...] = _rms(x_ref[...], g_ref[...]).astype(o_ref.dtype)


def _rmsnorm(x, g, out_dtype):
    m, d = x.shape
    return pl.pallas_call(
        _rmsnorm_kernel,
        grid=(m // ROW_TILE,),
        in_specs=[pl.BlockSpec((ROW_TILE, d), lambda i: (i, 0)),
                  pl.BlockSpec((1, d), lambda i: (0, 0))],
        out_specs=pl.BlockSpec((ROW_TILE, d), lambda i: (i, 0)),
        out_shape=jax.ShapeDtypeStruct((m, d), out_dtype),
        compiler_params=_params(("parallel",), 32),
        name="rmsnorm",
    )(x, g.reshape(1, d))


def _proj_kernel(x_ref, w_ref, cs_ref, o_ref):
    x = x_ref[...]
    n = w_ref.shape[1]
    for c in range(n // PROJ_CHUNK):
        sl = slice(c * PROJ_CHUNK, (c + 1) * PROJ_CHUNK)
        o_ref[:, sl] = (_dot(x, w_ref[:, sl]) * cs_ref[:, sl]).astype(o_ref.dtype)


def _proj_gate_kernel(x_ref, w_ref, cs_ref, wf_ref, o_ref, f_ref):
    _proj_kernel(x_ref, w_ref, cs_ref, o_ref)
    f_ref[...] = _dot(x_ref[...], wf_ref[...])


def _proj(xn, w, colscale, w_gate=None):
    m, k = xn.shape
    n = w.shape[1]
    in_specs = [pl.BlockSpec((ROW_TILE, k), lambda i: (i, 0)),
                pl.BlockSpec((k, n), lambda i: (0, 0)),
                pl.BlockSpec((1, n), lambda i: (0, 0))]
    out_specs = [pl.BlockSpec((ROW_TILE, n), lambda i: (i, 0))]
    out_shape = [jax.ShapeDtypeStruct((m, n), BF16)]
    args = [xn, w, colscale]
    body = _proj_kernel
    if w_gate is not None:
        in_specs.append(pl.BlockSpec((k, LANES), lambda i: (0, 0)))
        out_specs.append(pl.BlockSpec((ROW_TILE, LANES), lambda i: (i, 0)))
        out_shape.append(jax.ShapeDtypeStruct((m, LANES), F32))
        args.append(w_gate)
        body = _proj_gate_kernel
    return pl.pallas_call(
        body,
        grid=(m // ROW_TILE,),
        in_specs=in_specs,
        out_specs=out_specs,
        out_shape=out_shape,
        compiler_params=_params(("parallel",), 48),
        name="qkv_proj",
    )(*args)


def _softplus2(y):
    neg_abs = pltpu.bitcast(pltpu.bitcast(y, jnp.uint32) | jnp.uint32(0x80000000), F32)
    return jnp.maximum(y, 0.0) + jnp.log(1.0 + jnp.exp2(neg_abs)) * LOG2E


def _gate_cumsum_kernel(f_ref, b_ref, tri_ref, place_ref, o_ref):
    n_blocks = f_ref.shape[1] // CUM_BLOCK
    tri = tri_ref[...]
    lane = lax.broadcasted_iota(jnp.int32, (CUM_BLOCK, LANES), 1)
    ones = jnp.where((lane >= ONES_LANE) & (lane < ONES_LANE + GATE_PARTS), 1.0, 0.0)

    def body(r, carry):
        rows = pl.ds(pl.multiple_of(r * CUM_BLOCK, CUM_BLOCK), CUM_BLOCK)
        logit2 = (f_ref[0, rows, :] + b_ref[...]) * LOG2E
        ls = -_softplus2(-logit2)
        c2 = carry
        for piece in _split_bf16(ls, GATE_PARTS):
            c2 = c2 + _dot(tri, piece)
        operand = ones
        for i, piece in enumerate(_split_bf16(c2, GATE_PARTS)):
            operand = operand + _dot(piece, place_ref[i])
        o_ref[0, rows, :] = operand.astype(o_ref.dtype)
        return c2[CUM_BLOCK - 1:CUM_BLOCK, :]

    lax.fori_loop(0, n_blocks, body, jnp.zeros((1, LANES), F32))


def _gate_operand(f_logit, b_gate):
    b, s, _ = f_logit.shape
    idx = jnp.arange(CUM_BLOCK)
    tri = (idx[None, :] <= idx[:, None]).astype(BF16)
    lane = jnp.arange(LANES)
    place = jnp.stack([((lane[:, None] < N_HEADS) & (lane[None, :] == i * N_HEADS + lane[:, None]))
                       for i in range(GATE_PARTS)]).astype(BF16)
    return pl.pallas_call(
        _gate_cumsum_kernel,
        grid=(b,),
        in_specs=[pl.BlockSpec((1, s, LANES), lambda i: (i, 0, 0)),
                  pl.BlockSpec((1, LANES), lambda i: (0, 0)),
                  pl.BlockSpec((CUM_BLOCK, CUM_BLOCK), lambda i: (0, 0)),
                  pl.BlockSpec((GATE_PARTS, LANES, LANES), lambda i: (0, 0, 0))],
        out_specs=pl.BlockSpec((1, s, LANES), lambda i: (i, 0, 0)),
        out_shape=jax.ShapeDtypeStruct((b, s, LANES), BF16),
        compiler_params=_params(("parallel",), 32),
        name="gate_cumsum",
    )(f_logit, b_gate, tri, place)


def _stack_heads(q):
    lane = lax.broadcasted_iota(jnp.int32, q.shape, 1)
    zero = jnp.zeros_like(q)
    return jnp.concatenate([jnp.where(lane < HEAD_DIM, q, zero),
                            jnp.where(lane >= HEAD_DIM, q, zero)], axis=0)


def _causal(strict):
    shape = (TK, HEADS_PER_PAIR * TQ)
    key = lax.broadcasted_iota(jnp.int32, shape, 0)
    query = lax.broadcasted_iota(jnp.int32, shape, 1) & (TQ - 1)
    return key < query if strict else key <= query


def _rows(index, size):
    if isinstance(index, int):
        return pl.ds(index * size, size)
    return pl.ds(pl.multiple_of(index * size, size), size)


def _transpose_values(v_ref, store):
    row = lax.broadcasted_iota(jnp.int32, (LANES, LANES), 0)
    col = lax.broadcasted_iota(jnp.int32, (LANES, LANES), 1)
    eye = (row == col).astype(BF16)

    def body(j, carry):
        store(j, _dot_nt(eye, v_ref[0, _rows(j, VT_BLOCK), :]).astype(BF16))
        return carry

    lax.fori_loop(0, v_ref.shape[1] // VT_BLOCK, body, 0)


def _value_blocks(vt_ref, j, *lead):
    per_block = TK // VT_BLOCK
    return jnp.concatenate([vt_ref[(*lead, j * per_block + i)] for i in range(per_block)], axis=1)


def _pipelined_sweep(n_tiles, stages, descending):
    depth = len(stages)
    pairs = [(t, j) for t in range(n_tiles)
             for j in (range(t, -1, -1) if descending else range(t + 1))]
    n = len(pairs)
    assert n > depth

    def run(step_pairs, parity, masked, active):
        for s in active:
            slot = parity if s % 2 == 0 else 1 - parity
            tile, block = step_pairs[s]
            if s == 0:
                stages[0](tile, block, slot, masked)
            else:
                stages[s](tile, block, slot)

    def static_step(u):
        active = [s for s in range(depth) if 0 <= u - s < n]
        step_pairs = {s: pairs[u - s] for s in active}
        masked = 0 in step_pairs and step_pairs[0][0] == step_pairs[0][1]
        run(step_pairs, u & 1, masked, active)

    def advance(tile, block):
        if descending:
            last = block == 0
            return jnp.where(last, tile + 1, tile), jnp.where(last, tile + 1, block - 1)
        last = block == tile
        return jnp.where(last, tile + 1, tile), jnp.where(last, 0, block + 1)

    for u in range(depth - 1):
        static_step(u)

    def body(u, carry):
        step_pairs = {s: (carry[2 * s], carry[2 * s + 1]) for s in range(depth)}
        tile, block = step_pairs[0]
        for parity in range(2):
            for masked in (False, True):
                @pl.when(jnp.logical_and((u & 1) == parity, (block == tile) == masked))
                def _():
                    run(step_pairs, parity, masked, range(depth))
        return advance(tile, block) + tuple(carry[:2 * (depth - 1)])

    first = depth - 1
    init = ()
    for s in range(depth):
        init += tuple(jnp.int32(v) for v in pairs[first - s])
    lax.fori_loop(first, n, body, init)

    for u in range(n, n + depth - 1):
        static_step(u)


def _store_outputs(o_ref, n_tiles, normalized):
    def body(t, carry):
        acc_t = normalized(t)
        out_t = jnp.concatenate([acc_t[:HEAD_DIM, :TQ], acc_t[HEAD_DIM:, TQ:]], axis=0)
        o_ref[0, _rows(t, TQ), :] = out_t.T.astype(o_ref.dtype)
        return carry

    lax.fori_loop(0, n_tiles, body, 0)


def _fox_kernel(q_ref, k_ref, v_ref, gate_ref, perm_ref, o_ref,
                vt_ref, qa_ref, m_ref, acc_ref, s_buf, m_buf, a_buf):
    pair = pl.program_id(1)
    n_tiles = q_ref.shape[1] // TQ

    row = lax.broadcasted_iota(jnp.int32, (LANES, VT_BLOCK), 0)

    def store(j, vt):
        one = jnp.ones_like(vt)
        vt_ref[0, j] = jnp.where(row < HEAD_DIM, vt, one)
        vt_ref[1, j] = jnp.where(row >= HEAD_DIM, vt, one)

    _transpose_values(v_ref, store)

    lane = lax.broadcasted_iota(jnp.int32, (TQ, LANES), 1)

    def query_operand(t, carry):
        rows = _rows(t, TQ)
        gate_q = gate_ref[0, rows, :]
        q = q_ref[0, rows, :]
        zero = jnp.zeros_like(q)
        for hh in range(HEADS_PER_PAIR):
            h = HEADS_PER_PAIR * pair + hh
            minus = jnp.zeros((TQ, LANES), F32)
            for i in range(GATE_PARTS):
                minus = jnp.where(lane == i * N_HEADS + h, -1.0, minus)
            gate_lanes = (_dot(gate_q, perm_ref[h]) + minus).astype(BF16)
            feat = (lane < HEAD_DIM) if hh == 0 else (lane >= HEAD_DIM)
            qa_ref[t, hh * TQ:(hh + 1) * TQ, :] = jnp.concatenate(
                [jnp.where(feat, q, zero), gate_lanes], axis=1)
        return carry

    lax.fori_loop(0, n_tiles, query_operand, 0)
    m_ref[...] = jnp.full(m_ref.shape, -jnp.inf, F32)
    acc_ref[...] = jnp.zeros(acc_ref.shape, F32)

    def scores(t, j, slot, masked):
        rows = _rows(j, TK)
        k_aug = jnp.concatenate([k_ref[0, rows, :], gate_ref[0, rows, :]], axis=1)
        s = _dot_nt(k_aug, qa_ref[t])
        if masked:
            s = jnp.where(_causal(strict=False), s, -jnp.inf)
        m_old = m_ref[t]
        m_new = jnp.maximum(m_old, jnp.max(s, axis=0, keepdims=True))
        m_ref[t] = m_new
        m_buf[slot] = m_new
        a_buf[slot] = jnp.exp2(m_old - m_new)
        s_buf[slot] = s

    def accumulate(t, j, slot):
        prob = jnp.exp2(s_buf[slot] - m_buf[slot]).astype(BF16)
        pv = jnp.concatenate(
            [_dot(_value_blocks(vt_ref, j, hh), prob[:, hh * TQ:(hh + 1) * TQ])
             for hh in range(HEADS_PER_PAIR)], axis=1)
        acc_ref[t] = acc_ref[t] * a_buf[slot] + pv

    _pipelined_sweep(n_tiles, (scores, accumulate), descending=False)

    def normalized(t):
        acc = acc_ref[t]
        denom = jnp.concatenate(
            [jnp.broadcast_to(acc[HEAD_DIM:HEAD_DIM + 1, :TQ], (LANES, TQ)),
             jnp.broadcast_to(acc[0:1, TQ:], (LANES, TQ))], axis=1)
        return acc / denom

    _store_outputs(o_ref, n_tiles, normalized)


def _sb_kernel(q_ref, k_ref, v_ref, tt_ref, o_ref,
               vt_ref, qs_ref, r_ref, acc_ref, z_buf, w_buf):
    n_tiles = q_ref.shape[1] // TQ

    def store(j, vt):
        vt_ref[j] = vt

    _transpose_values(v_ref, store)

    def query_operand(t, carry):
        qs_ref[t] = _stack_heads(q_ref[0, _rows(t, TQ), :])
        return carry

    lax.fori_loop(0, n_tiles, query_operand, 0)
    r_ref[...] = jnp.zeros(r_ref.shape, F32)
    acc_ref[...] = jnp.zeros(acc_ref.shape, F32)

    n_sub = TK // SB_BLOCK

    def logits(t, j, slot, masked):
        z = _dot_nt(k_ref[0, _rows(j, TK), :], qs_ref[t])
        if masked:
            z = jnp.where(_causal(strict=True), z, -jnp.inf)
        z_buf[slot] = z

    def weights(t, j, slot):
        r = r_ref[t]
        for g in reversed(range(n_sub)):
            rows = slice(g * SB_BLOCK, (g + 1) * SB_BLOCK)
            z = z_buf[slot, rows, :]
            sp = _softplus2(z)
            sums = _dot(tt_ref[...], sp.astype(BF16))
            w_buf[slot, rows, :] = jnp.exp2(z - sp - sums[:SB_BLOCK] - r).astype(BF16)
            r = r + sums[SB_BLOCK:SB_BLOCK + 1]
        r_ref[t] = r

    def accumulate(t, j, slot):
        acc_ref[t] += _dot(_value_blocks(vt_ref, j), w_buf[slot])

    _pipelined_sweep(n_tiles, (logits, weights, accumulate), descending=True)
    _store_outputs(o_ref, n_tiles, lambda t: acc_ref[t])


def _attention_call(body, name, qkv, extra_inputs, extra_specs, scratch_shapes):
    b, s, _ = qkv.shape

    def column_block(offset):
        return pl.BlockSpec((1, s, LANES), lambda b, p: (b, 0, offset + p))

    return pl.pallas_call(
        body,
        grid=(b, N_HEAD_PAIRS),
        in_specs=[column_block(0), column_block(N_HEAD_PAIRS), column_block(2 * N_HEAD_PAIRS)
                  ] + extra_specs,
        out_specs=column_block(0),
        out_shape=jax.ShapeDtypeStruct((b, s, D_MODEL), BF16),
        scratch_shapes=scratch_shapes,
        compiler_params=_params(("parallel", "parallel"), 56),
        name=name,
    )(qkv, qkv, qkv, *extra_inputs)


def _fox_attention(qkv, gate):
    b, s, _ = qkv.shape
    n_tiles = s // TQ
    lane = jnp.arange(LANES)
    perm = jnp.stack([
        sum(((lane[:, None] == i * N_HEADS + h) & (lane[None, :] == ONES_LANE + i))
            for i in range(GATE_PARTS))
        for h in range(N_HEADS)]).astype(BF16)
    cols = HEADS_PER_PAIR * TQ
    return _attention_call(
        _fox_kernel, "fox_attention", qkv, [gate, perm],
        [pl.BlockSpec((1, s, LANES), lambda b, p: (b, 0, 0)),
         pl.BlockSpec((N_HEADS, LANES, LANES), lambda b, p: (0, 0, 0))],
        [pltpu.VMEM((HEADS_PER_PAIR, s // VT_BLOCK, LANES, VT_BLOCK), BF16),
         pltpu.VMEM((n_tiles, cols, 2 * LANES), BF16),
         pltpu.VMEM((n_tiles, 1, cols), F32),
         pltpu.VMEM((n_tiles, LANES, cols), F32),
         pltpu.VMEM((2, TK, cols), F32),
         pltpu.VMEM((2, 1, cols), F32),
         pltpu.VMEM((2, 1, cols), F32)])


def _sb_attention(qkv):
    b, s, _ = qkv.shape
    n_tiles = s // TQ
    idx = jnp.arange(SB_BLOCK)
    later = (idx[None, :] > idx[:, None]).astype(BF16)
    tt = jnp.concatenate([later, jnp.ones((SUBLANES, SB_BLOCK), BF16)], axis=0)
    cols = HEADS_PER_PAIR * TQ
    return _attention_call(
        _sb_kernel, "sb_attention", qkv, [tt],
        [pl.BlockSpec((SB_BLOCK + SUBLANES, SB_BLOCK), lambda b, p: (0, 0))],
        [pltpu.VMEM((s // VT_BLOCK, LANES, VT_BLOCK), BF16),
         pltpu.VMEM((n_tiles, cols, LANES), BF16),
         pltpu.VMEM((n_tiles, 1, cols), F32),
         pltpu.VMEM((n_tiles, LANES, cols), F32),
         pltpu.VMEM((2, TK, cols), F32),
         pltpu.VMEM((2, TK, cols), BF16)])


def _mm_resid_norm_kernel(a_ref, w_ref, h_ref, g_ref, hs_ref, xn_ref):
    h = _dot(a_ref[...], w_ref[...]) + h_ref[...]
    hs_ref[...] = h
    xn_ref[...] = _rms(h, g_ref[...]).astype(xn_ref.dtype)


def _mm_resid_final_kernel(a_ref, w_ref, h_ref, g_ref, o_ref):
    h = _dot(a_ref[...], w_ref[...]) + h_ref[...]
    o_ref[...] = _rms(h, g_ref[...]).astype(o_ref.dtype)


def _mm_resid_norm(a, w, hs, g, final=False):
    m, k = a.shape
    d = w.shape[1]
    row = lambda i: (i, 0)
    in_specs = [pl.BlockSpec((ROW_TILE, k), row),
                pl.BlockSpec((k, d), lambda i: (0, 0)),
                pl.BlockSpec((ROW_TILE, d), row),
                pl.BlockSpec((1, d), lambda i: (0, 0))]
    if final:
        body = _mm_resid_final_kernel
        out_specs = pl.BlockSpec((ROW_TILE, d), row)
        out_shape = jax.ShapeDtypeStruct((m, d), F32)
    else:
        body = _mm_resid_norm_kernel
        out_specs = [pl.BlockSpec((ROW_TILE, d), row), pl.BlockSpec((ROW_TILE, d), row)]
        out_shape = [jax.ShapeDtypeStruct((m, d), F32), jax.ShapeDtypeStruct((m, d), BF16)]
    return pl.pallas_call(
        body,
        grid=(m // ROW_TILE,),
        in_specs=in_specs,
        out_specs=out_specs,
        out_shape=out_shape,
        compiler_params=_params(("parallel",), 48),
        name="matmul_residual_norm",
    )(a, w, hs, g.reshape(1, d))


def _ffn_up_kernel(x_ref, w_ref, wc_ref, bc_ref, o_ref, carry_ref, *, tiles_per_seq):
    i = pl.program_id(0)

    @pl.when(i % tiles_per_seq == 0)
    def _():
        carry_ref[...] = jnp.zeros(carry_ref.shape, F32)

    x = x_ref[...]
    tm = x.shape[0]
    row = lax.broadcasted_iota(jnp.int32, (SUBLANES, 2 * FF_CHUNK), 0)

    def shifted(h, prev, shift):
        rolled = pltpu.roll(h, shift, 0)
        head = rolled[:SUBLANES]
        for i in range(shift):
            head = jnp.where(row == i, prev[SUBLANES - shift + i:SUBLANES - shift + i + 1], head)
        return jnp.concatenate([head, rolled[SUBLANES:]], axis=0)

    for c in range(N_FF_CHUNKS):
        h = _dot(x, w_ref[c])
        prev = carry_ref[c]
        h1 = shifted(h, prev, 1)
        h2 = shifted(h, prev, 2)
        wc = wc_ref[c]
        hc = bc_ref[c] + wc[0:1] * h2 + wc[1:2] * h1 + wc[2:3] * h
        u = hc[:, :FF_CHUNK]
        g = hc[:, FF_CHUNK:]
        act = g * u / (1.0 + jnp.exp(-g))
        o_ref[:, c * FF_CHUNK:(c + 1) * FF_CHUNK] = act.astype(o_ref.dtype)
        carry_ref[c] = h[tm - SUBLANES:tm]


def _ffn_up(xn, w_chunks, wc_chunks, bc_chunks, seq_len):
    m, k = xn.shape
    kern = functools.partial(_ffn_up_kernel, tiles_per_seq=seq_len // ROW_TILE)
    return pl.pallas_call(
        kern,
        grid=(m // ROW_TILE,),
        in_specs=[pl.BlockSpec((ROW_TILE, k), lambda i: (i, 0)),
                  pl.BlockSpec((N_FF_CHUNKS, k, 2 * FF_CHUNK), lambda i: (0, 0, 0)),
                  pl.BlockSpec((N_FF_CHUNKS, CONV_WIDTH, 2 * FF_CHUNK), lambda i: (0, 0, 0)),
                  pl.BlockSpec((N_FF_CHUNKS, 1, 2 * FF_CHUNK), lambda i: (0, 0, 0))],
        out_specs=pl.BlockSpec((ROW_TILE, D_FF), lambda i: (i, 0)),
        out_shape=jax.ShapeDtypeStruct((m, D_FF), BF16),
        scratch_shapes=[pltpu.VMEM((N_FF_CHUNKS, SUBLANES, 2 * FF_CHUNK), F32)],
        compiler_params=_params(("arbitrary",), 56),
        name="ffn_up_conv_gate",
    )(xn, w_chunks, wc_chunks, bc_chunks)


def _chunk_columns(t):
    r = t.shape[0]
    t = t.reshape(r, 2, N_FF_CHUNKS, FF_CHUNK)
    return t.transpose(2, 0, 1, 3).reshape(N_FF_CHUNKS, r, 2 * FF_CHUNK)


def kernel(x, attn_norm, ffn_norm, final_norm, fox_w_qkvf, fox_b_f, fox_w_o,
           sb_w_qkv, sb_w_o, ffn_w_up, ffn_w_conv, ffn_b_conv, ffn_w_down):
    b, s, d = x.shape
    m = b * s
    assert d == D_MODEL and s % TQ == 0 and s % ROW_TILE == 0 and TQ == TK

    colscale = jnp.concatenate([jnp.full((1, d), ATTN_SCALE * LOG2E, F32),
                                jnp.ones((1, 2 * d), F32)], axis=1)

    hs = x.reshape(m, d)
    xn = _rmsnorm(hs, attn_norm[0], BF16)
    out = None
    for i in range(DEPTH):
        j = i // 2
        if i % 2 == 0:
            w = fox_w_qkvf[j]
            w_gate = jnp.pad(w[:, 3 * d:], ((0, 0), (0, LANES - N_HEADS))).astype(BF16)
            qkv, f_logit = _proj(xn, w[:, :3 * d].astype(BF16), colscale, w_gate)
            b_gate = jnp.pad(fox_b_f[j], (0, LANES - N_HEADS)).reshape(1, LANES)
            gate = _gate_operand(f_logit.reshape(b, s, LANES), b_gate)
            o = _fox_attention(qkv.reshape(b, s, 3 * d), gate)
            w_o = fox_w_o[j]
        else:
            qkv = _proj(xn, sb_w_qkv[j].astype(BF16), colscale)[0]
            o = _sb_attention(qkv.reshape(b, s, 3 * d))
            w_o = sb_w_o[j]
        hs, xn = _mm_resid_norm(o.reshape(m, d), w_o.astype(BF16), hs, ffn_norm[i])
        act = _ffn_up(xn,
                      _chunk_columns(ffn_w_up[i]).astype(BF16),
                      _chunk_columns(ffn_w_conv[i]),
                      _chunk_columns(ffn_b_conv[i].reshape(1, -1)),
                      s)
        w_down = ffn_w_down[i].astype(BF16)
        if i + 1 < DEPTH:
            hs, xn = _mm_resid_norm(act, w_down, hs, attn_norm[i + 1])
        else:
            out = _mm_resid_norm(act, w_down, hs, final_norm, final=True)
    return out.reshape(b, s, d)
```

```python
import functools
import math

import jax
import jax.numpy as jnp
from jax import lax
from jax.experimental import pallas as pl
from jax.experimental.pallas import tpu as pltpu

D_MODEL = 1024
N_HEADS = 16
HEAD_DIM = D_MODEL // N_HEADS
D_FF = 2816
CONV_WIDTH = 3
DEPTH = 4
NORM_EPS = 1e-6
ATTN_SCALE = HEAD_DIM ** -0.5
LOG2E = math.log2(math.e)

F32 = jnp.float32
BF16 = jnp.bfloat16

LANES = 128
SUBLANES = 8
HEADS_PER_PAIR = LANES // HEAD_DIM
N_HEAD_PAIRS = N_HEADS // HEADS_PER_PAIR
ROW_TILE = 512
PROJ_CHUNK = 512
FF_CHUNK = 256
N_FF_CHUNKS = D_FF // FF_CHUNK
TQ = 512
TK = 512
VT_BLOCK = 256
CUM_BLOCK = 128
SB_BLOCK = 256
GATE_PARTS = 3
ONES_LANE = GATE_PARTS * N_HEADS
MIB = 2 ** 20


def _params(semantics, vmem_mib):
    return pltpu.CompilerParams(dimension_semantics=semantics,
                                vmem_limit_bytes=vmem_mib * MIB)


def _dot(a, b):
    return jnp.dot(a, b, preferred_element_type=F32)


def _dot_nt(a, b):
    return lax.dot_general(a, b, (((1,), (1,)), ((), ())), preferred_element_type=F32)


def _split_bf16(x, parts):
    pieces = []
    for _ in range(parts - 1):
        p = x.astype(BF16)
        pieces.append(p)
        x = x - p.astype(F32)
    pieces.append(x.astype(BF16))
    return pieces


def _rms(h, g):
    ms = jnp.mean(h * h, axis=-1, keepdims=True)
    return h * lax.rsqrt(ms + NORM_EPS) * g


def _rmsnorm_kernel(x_ref, g_ref, o_ref):
    o_ref[...] = _rms(x_ref[...], g_ref[...]).astype(o_ref.dtype)


def _rmsnorm(x, g, out_dtype):
    m, d = x.shape
    return pl.pallas_call(
        _rmsnorm_kernel,
        grid=(m // ROW_TILE,),
        in_specs=[pl.BlockSpec((ROW_TILE, d), lambda i: (i, 0)),
                  pl.BlockSpec((1, d), lambda i: (0, 0))],
        out_specs=pl.BlockSpec((ROW_TILE, d), lambda i: (i, 0)),
        out_shape=jax.ShapeDtypeStruct((m, d), out_dtype),
        compiler_params=_params(("parallel",), 32),
        name="rmsnorm",
    )(x, g.reshape(1, d))


def _proj_kernel(x_ref, w_ref, cs_ref, o_ref):
    x = x_ref[...]
    n = w_ref.shape[1]
    for c in range(n // PROJ_CHUNK):
        sl = slice(c * PROJ_CHUNK, (c + 1) * PROJ_CHUNK)
        o_ref[:, sl] = (_dot(x, w_ref[:, sl]) * cs_ref[:, sl]).astype(o_ref.dtype)


def _proj_gate_kernel(x_ref, w_ref, cs_ref, wf_ref, o_ref, f_ref):
    _proj_kernel(x_ref, w_ref, cs_ref, o_ref)
    f_ref[...] = _dot(x_ref[...], wf_ref[...])


def _proj(xn, w, colscale, w_gate=None):
    m, k = xn.shape
    n = w.shape[1]
    in_specs = [pl.BlockSpec((ROW_TILE, k), lambda i: (i, 0)),
                pl.BlockSpec((k, n), lambda i: (0, 0)),
                pl.BlockSpec((1, n), lambda i: (0, 0))]
    out_specs = [pl.BlockSpec((ROW_TILE, n), lambda i: (i, 0))]
    out_shape = [jax.ShapeDtypeStruct((m, n), BF16)]
    args = [xn, w, colscale]
    body = _proj_kernel
    if w_gate is not None:
        in_specs.append(pl.BlockSpec((k, LANES), lambda i: (0, 0)))
        out_specs.append(pl.BlockSpec((ROW_TILE, LANES), lambda i: (i, 0)))
        out_shape.append(jax.ShapeDtypeStruct((m, LANES), F32))
        args.append(w_gate)
        body = _proj_gate_kernel
    return pl.pallas_call(
        body,
        grid=(m // ROW_TILE,),
        in_specs=in_specs,
        out_specs=out_specs,
        out_shape=out_shape,
        compiler_params=_params(("parallel",), 48),
        name="qkv_proj",
    )(*args)


def _softplus2(y):
    neg_abs = pltpu.bitcast(pltpu.bitcast(y, jnp.uint32) | jnp.uint32(0x80000000), F32)
    return jnp.maximum(y, 0.0) + jnp.log(1.0 + jnp.exp2(neg_abs)) * LOG2E


def _gate_cumsum_kernel(f_ref, b_ref, tri_ref, place_ref, o_ref):
    n_blocks = f_ref.shape[1] // CUM_BLOCK
    tri = tri_ref[...]
    lane = lax.broadcasted_iota(jnp.int32, (CUM_BLOCK, LANES), 1)
    ones = jnp.where((lane >= ONES_LANE) & (lane < ONES_LANE + GATE_PARTS), 1.0, 0.0)

    def body(r, carry):
        rows = pl.ds(pl.multiple_of(r * CUM_BLOCK, CUM_BLOCK), CUM_BLOCK)
        logit2 = (f_ref[0, rows, :] + b_ref[...]) * LOG2E
        ls = -_softplus2(-logit2)
        c2 = carry
        for piece in _split_bf16(ls, GATE_PARTS):
            c2 = c2 + _dot(tri, piece)
        operand = ones
        for i, piece in enumerate(_split_bf16(c2, GATE_PARTS)):
            operand = operand + _dot(piece, place_ref[i])
        o_ref[0, rows, :] = operand.astype(o_ref.dtype)
        return c2[CUM_BLOCK - 1:CUM_BLOCK, :]

    lax.fori_loop(0, n_blocks, body, jnp.zeros((1, LANES), F32))


def _gate_operand(f_logit, b_gate):
    b, s, _ = f_logit.shape
    idx = jnp.arange(CUM_BLOCK)
    tri = (idx[None, :] <= idx[:, None]).astype(BF16)
    lane = jnp.arange(LANES)
    place = jnp.stack([((lane[:, None] < N_HEADS) & (lane[None, :] == i * N_HEADS + lane[:, None]))
                       for i in range(GATE_PARTS)]).astype(BF16)
    return pl.pallas_call(
        _gate_cumsum_kernel,
        grid=(b,),
        in_specs=[pl.BlockSpec((1, s, LANES), lambda i: (i, 0, 0)),
                  pl.BlockSpec((1, LANES), lambda i: (0, 0)),
                  pl.BlockSpec((CUM_BLOCK, CUM_BLOCK), lambda i: (0, 0)),
                  pl.BlockSpec((GATE_PARTS, LANES, LANES), lambda i: (0, 0, 0))],
        out_specs=pl.BlockSpec((1, s, LANES), lambda i: (i, 0, 0)),
        out_shape=jax.ShapeDtypeStruct((b, s, LANES), BF16),
        compiler_params=_params(("parallel",), 32),
        name="gate_cumsum",
    )(f_logit, b_gate, tri, place)


def _stack_heads(q):
    lane = lax.broadcasted_iota(jnp.int32, q.shape, 1)
    zero = jnp.zeros_like(q)
    return jnp.concatenate([jnp.where(lane < HEAD_DIM, q, zero),
                            jnp.where(lane >= HEAD_DIM, q, zero)], axis=0)


def _causal(strict):
    shape = (TK, HEADS_PER_PAIR * TQ)
    key = lax.broadcasted_iota(jnp.int32, shape, 0)
    query = lax.broadcasted_iota(jnp.int32, shape, 1) & (TQ - 1)
    return key < query if strict else key <= query


def _rows(index, size):
    if isinstance(index, int):
        return pl.ds(index * size, size)
    return pl.ds(pl.multiple_of(index * size, size), size)


def _transpose_values(v_ref, store):
    row = lax.broadcasted_iota(jnp.int32, (LANES, LANES), 0)
    col = lax.broadcasted_iota(jnp.int32, (LANES, LANES), 1)
    eye = (row == col).astype(BF16)

    def body(j, carry):
        store(j, _dot_nt(eye, v_ref[0, _rows(j, VT_BLOCK), :]).astype(BF16))
        return carry

    lax.fori_loop(0, v_ref.shape[1] // VT_BLOCK, body, 0)


def _value_blocks(vt_ref, j, *lead):
    per_block = TK // VT_BLOCK
    return jnp.concatenate([vt_ref[(*lead, j * per_block + i)] for i in range(per_block)], axis=1)


def _pipelined_sweep(n_tiles, stages, descending):
    depth = len(stages)
    pairs = [(t, j) for t in range(n_tiles)
             for j in (range(t, -1, -1) if descending else range(t + 1))]
    n = len(pairs)
    assert n > depth

    def run(step_pairs, parity, masked, active):
        for s in active:
            slot = parity if s % 2 == 0 else 1 - parity
            tile, block = step_pairs[s]
            if s == 0:
                stages[0](tile, block, slot, masked)
            else:
                stages[s](tile, block, slot)

    def static_step(u):
        active = [s for s in range(depth) if 0 <= u - s < n]
        step_pairs = {s: pairs[u - s] for s in active}
        masked = 0 in step_pairs and step_pairs[0][0] == step_pairs[0][1]
        run(step_pairs, u & 1, masked, active)

    def advance(tile, block):
        if descending:
            last = block == 0
            return jnp.where(last, tile + 1, tile), jnp.where(last, tile + 1, block - 1)
        last = block == tile
        return jnp.where(last, tile + 1, tile), jnp.where(last, 0, block + 1)

    start = depth - 1 + (n - depth + 1) % 2
    for u in range(start):
        static_step(u)

    def on_diagonal(pair):
        return pair[0] == pair[1]

    mask_cases = sorted({(on_diagonal(pairs[u]), on_diagonal(pairs[u + 1]))
                         for u in range(start, n, 2)})

    def body(i, carry):
        first_pairs = {s: (carry[2 * s], carry[2 * s + 1]) for s in range(depth)}
        second_pairs = {0: advance(*first_pairs[0])}
        second_pairs.update({s: first_pairs[s - 1] for s in range(1, depth)})
        for masked in mask_cases:
            @pl.when(jnp.logical_and(on_diagonal(first_pairs[0]) == masked[0],
                                     on_diagonal(second_pairs[0]) == masked[1]))
            def _():
                run(first_pairs, start & 1, masked[0], range(depth))
                run(second_pairs, 1 - (start & 1), masked[1], range(depth))
        return advance(*second_pairs[0]) + second_pairs[0] + tuple(carry[:2 * (depth - 2)])

    init = ()
    for s in range(depth):
        init += tuple(jnp.int32(v) for v in pairs[start - s])
    lax.fori_loop(0, (n - start) // 2, body, init)

    for u in range(n, n + depth - 1):
        static_step(u)


def _store_outputs(o_ref, n_tiles, normalized):
    def body(t, carry):
        acc_t = normalized(t)
        out_t = jnp.concatenate([acc_t[:HEAD_DIM, :TQ], acc_t[HEAD_DIM:, TQ:]], axis=0)
        o_ref[0, _rows(t, TQ), :] = out_t.T.astype(o_ref.dtype)
        return carry

    lax.fori_loop(0, n_tiles, body, 0)


def _fox_kernel(q_ref, k_ref, v_ref, gate_ref, perm_ref, o_ref,
                vt_ref, qa_ref, m_ref, acc_ref, s_buf, m_buf, a_buf):
    pair = pl.program_id(1)
    n_tiles = q_ref.shape[1] // TQ

    row = lax.broadcasted_iota(jnp.int32, (LANES, VT_BLOCK), 0)

    def store(j, vt):
        one = jnp.ones_like(vt)
        vt_ref[0, j] = jnp.where(row < HEAD_DIM, vt, one)
        vt_ref[1, j] = jnp.where(row >= HEAD_DIM, vt, one)

    _transpose_values(v_ref, store)

    lane = lax.broadcasted_iota(jnp.int32, (TQ, LANES), 1)

    def query_operand(t, carry):
        rows = _rows(t, TQ)
        gate_q = gate_ref[0, rows, :]
        q = q_ref[0, rows, :]
        zero = jnp.zeros_like(q)
        for hh in range(HEADS_PER_PAIR):
            h = HEADS_PER_PAIR * pair + hh
            minus = jnp.zeros((TQ, LANES), F32)
            for i in range(GATE_PARTS):
                minus = jnp.where(lane == i * N_HEADS + h, -1.0, minus)
            gate_lanes = (_dot(gate_q, perm_ref[h]) + minus).astype(BF16)
            feat = (lane < HEAD_DIM) if hh == 0 else (lane >= HEAD_DIM)
            qa_ref[t, hh * TQ:(hh + 1) * TQ, :] = jnp.concatenate(
                [jnp.where(feat, q, zero), gate_lanes], axis=1)
        return carry

    lax.fori_loop(0, n_tiles, query_operand, 0)
    m_ref[...] = jnp.full(m_ref.shape, -jnp.inf, F32)
    acc_ref[...] = jnp.zeros(acc_ref.shape, F32)

    def scores(t, j, slot, masked):
        rows = _rows(j, TK)
        k_aug = jnp.concatenate([k_ref[0, rows, :], gate_ref[0, rows, :]], axis=1)
        s = _dot_nt(k_aug, qa_ref[t])
        if masked:
            s = jnp.where(_causal(strict=False), s, -jnp.inf)
        m_old = m_ref[t]
        m_new = jnp.maximum(m_old, jnp.max(s, axis=0, keepdims=True))
        m_ref[t] = m_new
        m_buf[slot] = m_new
        a_buf[slot] = jnp.exp2(m_old - m_new)
        s_buf[slot] = s

    def accumulate(t, j, slot):
        prob = jnp.exp2(s_buf[slot] - m_buf[slot]).astype(BF16)
        pv = jnp.concatenate(
            [_dot(_value_blocks(vt_ref, j, hh), prob[:, hh * TQ:(hh + 1) * TQ])
             for hh in range(HEADS_PER_PAIR)], axis=1)
        acc_ref[t] = acc_ref[t] * a_buf[slot] + pv

    _pipelined_sweep(n_tiles, (scores, accumulate), descending=False)

    def normalized(t):
        acc = acc_ref[t]
        denom = jnp.concatenate(
            [jnp.broadcast_to(acc[HEAD_DIM:HEAD_DIM + 1, :TQ], (LANES, TQ)),
             jnp.broadcast_to(acc[0:1, TQ:], (LANES, TQ))], axis=1)
        return acc / denom

    _store_outputs(o_ref, n_tiles, normalized)


def _sb_kernel(q_ref, k_ref, v_ref, tt_ref, o_ref,
               vt_ref, qs_ref, r_ref, acc_ref, z_buf, w_buf):
    n_tiles = q_ref.shape[1] // TQ

    def store(j, vt):
        vt_ref[j] = vt

    _transpose_values(v_ref, store)

    def query_operand(t, carry):
        qs_ref[t] = _stack_heads(q_ref[0, _rows(t, TQ), :])
        return carry

    lax.fori_loop(0, n_tiles, query_operand, 0)
    r_ref[...] = jnp.zeros(r_ref.shape, F32)
    acc_ref[...] = jnp.zeros(acc_ref.shape, F32)

    n_sub = TK // SB_BLOCK

    def logits(t, j, slot, masked):
        z = _dot_nt(k_ref[0, _rows(j, TK), :], qs_ref[t])
        if masked:
            z = jnp.where(_causal(strict=True), z, -jnp.inf)
        z_buf[slot] = z

    def weights(t, j, slot):
        r = r_ref[t]
        for g in reversed(range(n_sub)):
            rows = slice(g * SB_BLOCK, (g + 1) * SB_BLOCK)
            z = z_buf[slot, rows, :]
            sp = _softplus2(z)
            sums = _dot(tt_ref[...], sp.astype(BF16))
            w_buf[slot, rows, :] = jnp.exp2(z - sp - sums[:SB_BLOCK] - r).astype(BF16)
            r = r + sums[SB_BLOCK:SB_BLOCK + 1]
        r_ref[t] = r

    def accumulate(t, j, slot):
        acc_ref[t] += _dot(_value_blocks(vt_ref, j), w_buf[slot])

    _pipelined_sweep(n_tiles, (logits, weights, accumulate), descending=True)
    _store_outputs(o_ref, n_tiles, lambda t: acc_ref[t])


def _attention_call(body, name, qkv, extra_inputs, extra_specs, scratch_shapes):
    b, s, _ = qkv.shape

    def column_block(offset):
        return pl.BlockSpec((1, s, LANES), lambda b, p: (b, 0, offset + p))

    return pl.pallas_call(
        body,
        grid=(b, N_HEAD_PAIRS),
        in_specs=[column_block(0), column_block(N_HEAD_PAIRS), column_block(2 * N_HEAD_PAIRS)
                  ] + extra_specs,
        out_specs=column_block(0),
        out_shape=jax.ShapeDtypeStruct((b, s, D_MODEL), BF16),
        scratch_shapes=scratch_shapes,
        compiler_params=_params(("parallel", "parallel"), 56),
        name=name,
    )(qkv, qkv, qkv, *extra_inputs)


def _fox_attention(qkv, gate):
    b, s, _ = qkv.shape
    n_tiles = s // TQ
    lane = jnp.arange(LANES)
    perm = jnp.stack([
        sum(((lane[:, None] == i * N_HEADS + h) & (lane[None, :] == ONES_LANE + i))
            for i in range(GATE_PARTS))
        for h in range(N_HEADS)]).astype(BF16)
    cols = HEADS_PER_PAIR * TQ
    return _attention_call(
        _fox_kernel, "fox_attention", qkv, [gate, perm],
        [pl.BlockSpec((1, s, LANES), lambda b, p: (b, 0, 0)),
         pl.BlockSpec((N_HEADS, LANES, LANES), lambda b, p: (0, 0, 0))],
        [pltpu.VMEM((HEADS_PER_PAIR, s // VT_BLOCK, LANES, VT_BLOCK), BF16),
         pltpu.VMEM((n_tiles, cols, 2 * LANES), BF16),
         pltpu.VMEM((n_tiles, 1, cols), F32),
         pltpu.VMEM((n_tiles, LANES, cols), F32),
         pltpu.VMEM((2, TK, cols), F32),
         pltpu.VMEM((2, 1, cols), F32),
         pltpu.VMEM((2, 1, cols), F32)])


def _sb_attention(qkv):
    b, s, _ = qkv.shape
    n_tiles = s // TQ
    idx = jnp.arange(SB_BLOCK)
    later = (idx[None, :] > idx[:, None]).astype(BF16)
    tt = jnp.concatenate([later, jnp.ones((SUBLANES, SB_BLOCK), BF16)], axis=0)
    cols = HEADS_PER_PAIR * TQ
    return _attention_call(
        _sb_kernel, "sb_attention", qkv, [tt],
        [pl.BlockSpec((SB_BLOCK + SUBLANES, SB_BLOCK), lambda b, p: (0, 0))],
        [pltpu.VMEM((s // VT_BLOCK, LANES, VT_BLOCK), BF16),
         pltpu.VMEM((n_tiles, cols, LANES), BF16),
         pltpu.VMEM((n_tiles, 1, cols), F32),
         pltpu.VMEM((n_tiles, LANES, cols), F32),
         pltpu.VMEM((2, TK, cols), F32),
         pltpu.VMEM((2, TK, cols), BF16)])


def _mm_resid_norm_kernel(a_ref, w_ref, h_ref, g_ref, hs_ref, xn_ref):
    h = _dot(a_ref[...], w_ref[...]) + h_ref[...]
    hs_ref[...] = h
    xn_ref[...] = _rms(h, g_ref[...]).astype(xn_ref.dtype)


def _mm_resid_final_kernel(a_ref, w_ref, h_ref, g_ref, o_ref):
    h = _dot(a_ref[...], w_ref[...]) + h_ref[...]
    o_ref[...] = _rms(h, g_ref[...]).astype(o_ref.dtype)


def _mm_resid_norm(a, w, hs, g, final=False):
    m, k = a.shape
    d = w.shape[1]
    row = lambda i: (i, 0)
    in_specs = [pl.BlockSpec((ROW_TILE, k), row),
                pl.BlockSpec((k, d), lambda i: (0, 0)),
                pl.BlockSpec((ROW_TILE, d), row),
                pl.BlockSpec((1, d), lambda i: (0, 0))]
    if final:
        body = _mm_resid_final_kernel
        out_specs = pl.BlockSpec((ROW_TILE, d), row)
        out_shape = jax.ShapeDtypeStruct((m, d), F32)
    else:
        body = _mm_resid_norm_kernel
        out_specs = [pl.BlockSpec((ROW_TILE, d), row), pl.BlockSpec((ROW_TILE, d), row)]
        out_shape = [jax.ShapeDtypeStruct((m, d), F32), jax.ShapeDtypeStruct((m, d), BF16)]
    return pl.pallas_call(
        body,
        grid=(m // ROW_TILE,),
        in_specs=in_specs,
        out_specs=out_specs,
        out_shape=out_shape,
        compiler_params=_params(("parallel",), 48),
        name="matmul_residual_norm",
    )(a, w, hs, g.reshape(1, d))


def _ffn_up_kernel(x_ref, w_ref, wc_ref, bc_ref, o_ref, carry_ref, *, tiles_per_seq):
    i = pl.program_id(0)

    @pl.when(i % tiles_per_seq == 0)
    def _():
        carry_ref[...] = jnp.zeros(carry_ref.shape, F32)

    x = x_ref[...]
    tm = x.shape[0]
    row = lax.broadcasted_iota(jnp.int32, (SUBLANES, 2 * FF_CHUNK), 0)

    def shifted(h, prev, shift):
        rolled = pltpu.roll(h, shift, 0)
        head = rolled[:SUBLANES]
        for i in range(shift):
            head = jnp.where(row == i, prev[SUBLANES - shift + i:SUBLANES - shift + i + 1], head)
        return jnp.concatenate([head, rolled[SUBLANES:]], axis=0)

    for c in range(N_FF_CHUNKS):
        h = _dot(x, w_ref[c])
        prev = carry_ref[c]
        h1 = shifted(h, prev, 1)
        h2 = shifted(h, prev, 2)
        wc = wc_ref[c]
        hc = bc_ref[c] + wc[0:1] * h2 + wc[1:2] * h1 + wc[2:3] * h
        u = hc[:, :FF_CHUNK]
        g = hc[:, FF_CHUNK:]
        act = g * u / (1.0 + jnp.exp(-g))
        o_ref[:, c * FF_CHUNK:(c + 1) * FF_CHUNK] = act.astype(o_ref.dtype)
        carry_ref[c] = h[tm - SUBLANES:tm]


def _ffn_up(xn, w_chunks, wc_chunks, bc_chunks, seq_len):
    m, k = xn.shape
    kern = functools.partial(_ffn_up_kernel, tiles_per_seq=seq_len // ROW_TILE)
    return pl.pallas_call(
        kern,
        grid=(m // ROW_TILE,),
        in_specs=[pl.BlockSpec((ROW_TILE, k), lambda i: (i, 0)),
                  pl.BlockSpec((N_FF_CHUNKS, k, 2 * FF_CHUNK), lambda i: (0, 0, 0)),
                  pl.BlockSpec((N_FF_CHUNKS, CONV_WIDTH, 2 * FF_CHUNK), lambda i: (0, 0, 0)),
                  pl.BlockSpec((N_FF_CHUNKS, 1, 2 * FF_CHUNK), lambda i: (0, 0, 0))],
        out_specs=pl.BlockSpec((ROW_TILE, D_FF), lambda i: (i, 0)),
        out_shape=jax.ShapeDtypeStruct((m, D_FF), BF16),
        scratch_shapes=[pltpu.VMEM((N_FF_CHUNKS, SUBLANES, 2 * FF_CHUNK), F32)],
        compiler_params=_params(("arbitrary",), 56),
        name="ffn_up_conv_gate",
    )(xn, w_chunks, wc_chunks, bc_chunks)


def _chunk_columns(t):
    r = t.shape[0]
    t = t.reshape(r, 2, N_FF_CHUNKS, FF_CHUNK)
    return t.transpose(2, 0, 1, 3).reshape(N_FF_CHUNKS, r, 2 * FF_CHUNK)


def kernel(x, attn_norm, ffn_norm, final_norm, fox_w_qkvf, fox_b_f, fox_w_o,
           sb_w_qkv, sb_w_o, ffn_w_up, ffn_w_conv, ffn_b_conv, ffn_w_down):
    b, s, d = x.shape
    m = b * s
    assert d == D_MODEL and s % TQ == 0 and s % ROW_TILE == 0 and TQ == TK

    colscale = jnp.concatenate([jnp.full((1, d), ATTN_SCALE * LOG2E, F32),
                                jnp.ones((1, 2 * d), F32)], axis=1)

    hs = x.reshape(m, d)
    xn = _rmsnorm(hs, attn_norm[0], BF16)
    out = None
    for i in range(DEPTH):
        j = i // 2
        if i % 2 == 0:
            w = fox_w_qkvf[j]
            w_gate = jnp.pad(w[:, 3 * d:], ((0, 0), (0, LANES - N_HEADS))).astype(BF16)
            qkv, f_logit = _proj(xn, w[:, :3 * d].astype(BF16), colscale, w_gate)
            b_gate = jnp.pad(fox_b_f[j], (0, LANES - N_HEADS)).reshape(1, LANES)
            gate = _gate_operand(f_logit.reshape(b, s, LANES), b_gate)
            o = _fox_attention(qkv.reshape(b, s, 3 * d), gate)
            w_o = fox_w_o[j]
        else:
            qkv = _proj(xn, sb_w_qkv[j].astype(BF16), colscale)[0]
            o = _sb_attention(qkv.reshape(b, s, 3 * d))
            w_o = sb_w_o[j]
        hs, xn = _mm_resid_norm(o.reshape(m, d), w_o.astype(BF16), hs, ffn_norm[i])
        act = _ffn_up(xn,
                      _chunk_columns(ffn_w_up[i]).astype(BF16),
                      _chunk_columns(ffn_w_conv[i]),
                      _chunk_columns(ffn_b_conv[i].reshape(1, -1)),
                      s)
        w_down = ffn_w_down[i].astype(BF16)
        if i + 1 < DEPTH:
            hs, xn = _mm_resid_norm(act, w_down, hs, attn_norm[i + 1])
        else:
            out = _mm_resid_norm(act, w_down, hs, final_norm, final=True)
    return out.reshape(b, s, d)
```

```python
import functools
import math

import jax
import jax.numpy as jnp
from jax import lax
from jax.experimental import pallas as pl
from jax.experimental.pallas import tpu as pltpu

D_MODEL = 1024
N_HEADS = 16
HEAD_DIM = D_MODEL // N_HEADS
D_FF = 2816
CONV_WIDTH = 3
DEPTH = 4
NORM_EPS = 1e-6
ATTN_SCALE = HEAD_DIM ** -0.5
LOG2E = math.log2(math.e)

F32 = jnp.float32
BF16 = jnp.bfloat16

LANES = 128
SUBLANES = 8
HEADS_PER_PAIR = LANES // HEAD_DIM
N_HEAD_PAIRS = N_HEADS // HEADS_PER_PAIR
ROW_TILE = 512
PROJ_CHUNK = 512
FF_CHUNK = 256
N_FF_CHUNKS = D_FF // FF_CHUNK
TQ = 512
TK = 512
VT_BLOCK = 256
VT_GROUP = 1024
PIPELINE_UNROLL = 4
CUM_BLOCK = 128
SB_BLOCK = 256
GATE_PARTS = 3
ONES_LANE = GATE_PARTS * N_HEADS
MIB = 2 ** 20


def _params(semantics, vmem_mib):
    return pltpu.CompilerParams(dimension_semantics=semantics,
                                vmem_limit_bytes=vmem_mib * MIB)


def _dot(a, b):
    return jnp.dot(a, b, preferred_element_type=F32)


def _dot_nt(a, b):
    return lax.dot_general(a, b, (((1,), (1,)), ((), ())), preferred_element_type=F32)


def _split_bf16(x, parts):
    pieces = []
    for _ in range(parts - 1):
        p = x.astype(BF16)
        pieces.append(p)
        x = x - p.astype(F32)
    pieces.append(x.astype(BF16))
    return pieces


def _rms(h, g):
    ms = jnp.mean(h * h, axis=-1, keepdims=True)
    return h * lax.rsqrt(ms + NORM_EPS) * g


def _rmsnorm_kernel(x_ref, g_ref, o_ref):
    o_ref[...] = _rms(x_ref[...], g_ref[...]).astype(o_ref.dtype)


def _rmsnorm(x, g, out_dtype):
    m, d = x.shape
    return pl.pallas_call(
        _rmsnorm_kernel,
        grid=(m // ROW_TILE,),
        in_specs=[pl.BlockSpec((ROW_TILE, d), lambda i: (i, 0)),
                  pl.BlockSpec((1, d), lambda i: (0, 0))],
        out_specs=pl.BlockSpec((ROW_TILE, d), lambda i: (i, 0)),
        out_shape=jax.ShapeDtypeStruct((m, d), out_dtype),
        compiler_params=_params(("parallel",), 32),
        name="rmsnorm",
    )(x, g.reshape(1, d))


def _proj_kernel(x_ref, w_ref, cs_ref, o_ref):
    x = x_ref[...]
    n = w_ref.shape[1]
    for c in range(n // PROJ_CHUNK):
        sl = slice(c * PROJ_CHUNK, (c + 1) * PROJ_CHUNK)
        o_ref[:, sl] = (_dot(x, w_ref[:, sl]) * cs_ref[:, sl]).astype(o_ref.dtype)


def _proj_gate_kernel(x_ref, w_ref, cs_ref, wf_ref, o_ref, f_ref):
    _proj_kernel(x_ref, w_ref, cs_ref, o_ref)
    f_ref[...] = _dot(x_ref[...], wf_ref[...])


def _proj(xn, w, colscale, w_gate=None):
    m, k = xn.shape
    n = w.shape[1]
    in_specs = [pl.BlockSpec((ROW_TILE, k), lambda i: (i, 0)),
                pl.BlockSpec((k, n), lambda i: (0, 0)),
                pl.BlockSpec((1, n), lambda i: (0, 0))]
    out_specs = [pl.BlockSpec((ROW_TILE, n), lambda i: (i, 0))]
    out_shape = [jax.ShapeDtypeStruct((m, n), BF16)]
    args = [xn, w, colscale]
    body = _proj_kernel
    if w_gate is not None:
        in_specs.append(pl.BlockSpec((k, LANES), lambda i: (0, 0)))
        out_specs.append(pl.BlockSpec((ROW_TILE, LANES), lambda i: (i, 0)))
        out_shape.append(jax.ShapeDtypeStruct((m, LANES), F32))
        args.append(w_gate)
        body = _proj_gate_kernel
    return pl.pallas_call(
        body,
        grid=(m // ROW_TILE,),
        in_specs=in_specs,
        out_specs=out_specs,
        out_shape=out_shape,
        compiler_params=_params(("parallel",), 48),
        name="qkv_proj",
    )(*args)


def _softplus2(y):
    neg_abs = pltpu.bitcast(pltpu.bitcast(y, jnp.uint32) | jnp.uint32(0x80000000), F32)
    return jnp.maximum(y, 0.0) + jnp.log(1.0 + jnp.exp2(neg_abs)) * LOG2E


def _gate_cumsum_kernel(f_ref, b_ref, tri_ref, place_ref, o_ref):
    n_blocks = f_ref.shape[1] // CUM_BLOCK
    tri = tri_ref[...]
    lane = lax.broadcasted_iota(jnp.int32, (CUM_BLOCK, LANES), 1)
    ones = jnp.where((lane >= ONES_LANE) & (lane < ONES_LANE + GATE_PARTS), 1.0, 0.0)

    def body(r, carry):
        rows = pl.ds(pl.multiple_of(r * CUM_BLOCK, CUM_BLOCK), CUM_BLOCK)
        logit2 = (f_ref[0, rows, :] + b_ref[...]) * LOG2E
        ls = -_softplus2(-logit2)
        c2 = carry
        for piece in _split_bf16(ls, GATE_PARTS):
            c2 = c2 + _dot(tri, piece)
        operand = ones
        for i, piece in enumerate(_split_bf16(c2, GATE_PARTS)):
            operand = operand + _dot(piece, place_ref[i])
        o_ref[0, rows, :] = operand.astype(o_ref.dtype)
        return c2[CUM_BLOCK - 1:CUM_BLOCK, :]

    lax.fori_loop(0, n_blocks, body, jnp.zeros((1, LANES), F32))


def _gate_operand(f_logit, b_gate):
    b, s, _ = f_logit.shape
    idx = jnp.arange(CUM_BLOCK)
    tri = (idx[None, :] <= idx[:, None]).astype(BF16)
    lane = jnp.arange(LANES)
    place = jnp.stack([((lane[:, None] < N_HEADS) & (lane[None, :] == i * N_HEADS + lane[:, None]))
                       for i in range(GATE_PARTS)]).astype(BF16)
    return pl.pallas_call(
        _gate_cumsum_kernel,
        grid=(b,),
        in_specs=[pl.BlockSpec((1, s, LANES), lambda i: (i, 0, 0)),
                  pl.BlockSpec((1, LANES), lambda i: (0, 0)),
                  pl.BlockSpec((CUM_BLOCK, CUM_BLOCK), lambda i: (0, 0)),
                  pl.BlockSpec((GATE_PARTS, LANES, LANES), lambda i: (0, 0, 0))],
        out_specs=pl.BlockSpec((1, s, LANES), lambda i: (i, 0, 0)),
        out_shape=jax.ShapeDtypeStruct((b, s, LANES), BF16),
        compiler_params=_params(("parallel",), 32),
        name="gate_cumsum",
    )(f_logit, b_gate, tri, place)


def _stack_heads(q):
    lane = lax.broadcasted_iota(jnp.int32, q.shape, 1)
    zero = jnp.zeros_like(q)
    return jnp.concatenate([jnp.where(lane < HEAD_DIM, q, zero),
                            jnp.where(lane >= HEAD_DIM, q, zero)], axis=0)


def _causal(strict):
    shape = (TK, HEADS_PER_PAIR * TQ)
    key = lax.broadcasted_iota(jnp.int32, shape, 0)
    query = lax.broadcasted_iota(jnp.int32, shape, 1) & (TQ - 1)
    return key < query if strict else key <= query


def _rows(index, size):
    if isinstance(index, int):
        return pl.ds(index * size, size)
    return pl.ds(pl.multiple_of(index * size, size), size)


def _transpose_values(v_ref, store):
    row = lax.broadcasted_iota(jnp.int32, (LANES, LANES), 0)
    col = lax.broadcasted_iota(jnp.int32, (LANES, LANES), 1)
    eye = (row == col).astype(BF16)

    group = VT_GROUP // VT_BLOCK

    def body(i, carry):
        vt = _dot_nt(eye, v_ref[0, _rows(i, VT_GROUP), :]).astype(BF16)
        for g in range(group):
            store(i * group + g, vt[:, g * VT_BLOCK:(g + 1) * VT_BLOCK])
        return carry

    lax.fori_loop(0, v_ref.shape[1] // VT_GROUP, body, 0)


def _value_blocks(vt_ref, j, *lead):
    per_block = TK // VT_BLOCK
    return jnp.concatenate([vt_ref[(*lead, j * per_block + i)] for i in range(per_block)], axis=1)


def _pipelined_sweep(n_tiles, stages, descending):
    depth = len(stages)
    pairs = [(t, j) for t in range(n_tiles)
             for j in (range(t, -1, -1) if descending else range(t + 1))]
    n = len(pairs)
    assert n >= depth - 1 + PIPELINE_UNROLL and PIPELINE_UNROLL % 2 == 0

    def run(step_pairs, parity, masked, active):
        for s in active:
            slot = parity if s % 2 == 0 else 1 - parity
            tile, block = step_pairs[s]
            if s == 0:
                stages[0](tile, block, slot, masked)
            else:
                stages[s](tile, block, slot)

    def static_step(u):
        active = [s for s in range(depth) if 0 <= u - s < n]
        step_pairs = {s: pairs[u - s] for s in active}
        masked = 0 in step_pairs and step_pairs[0][0] == step_pairs[0][1]
        run(step_pairs, u & 1, masked, active)

    def advance(tile, block):
        if descending:
            last = block == 0
            return jnp.where(last, tile + 1, tile), jnp.where(last, tile + 1, block - 1)
        last = block == tile
        return jnp.where(last, tile + 1, tile), jnp.where(last, 0, block + 1)

    start = depth - 1 + (n - depth + 1) % PIPELINE_UNROLL
    for u in range(start):
        static_step(u)

    def on_diagonal(pair):
        return pair[0] == pair[1]

    mask_cases = sorted({tuple(on_diagonal(pairs[u + i]) for i in range(PIPELINE_UNROLL))
                         for u in range(start, n, PIPELINE_UNROLL)})

    def body(_, carry):
        window = [(carry[2 * s], carry[2 * s + 1]) for s in range(depth)]
        for _ in range(PIPELINE_UNROLL - 1):
            window.insert(0, advance(*window[0]))
        heads = window[:PIPELINE_UNROLL][::-1]
        for masked in mask_cases:
            cond = on_diagonal(heads[0]) == masked[0]
            for i in range(1, PIPELINE_UNROLL):
                cond = jnp.logical_and(cond, on_diagonal(heads[i]) == masked[i])

            @pl.when(cond)
            def _():
                for i in range(PIPELINE_UNROLL):
                    offset = PIPELINE_UNROLL - 1 - i
                    step_pairs = {s: window[offset + s] for s in range(depth)}
                    run(step_pairs, (start + i) & 1, masked[i], range(depth))
        window.insert(0, advance(*window[0]))
        return tuple(v for pair in window[:depth] for v in pair)

    init = ()
    for s in range(depth):
        init += tuple(jnp.int32(v) for v in pairs[start - s])
    lax.fori_loop(0, (n - start) // PIPELINE_UNROLL, body, init)

    for u in range(n, n + depth - 1):
        static_step(u)


def _store_outputs(o_ref, n_tiles, normalized):
    def body(t, carry):
        acc_t = normalized(t)
        out_t = jnp.concatenate([acc_t[:HEAD_DIM, :TQ], acc_t[HEAD_DIM:, TQ:]], axis=0)
        o_ref[0, _rows(t, TQ), :] = out_t.T.astype(o_ref.dtype)
        return carry

    lax.fori_loop(0, n_tiles, body, 0)


def _fox_kernel(q_ref, k_ref, v_ref, gate_ref, perm_ref, o_ref,
                vt_ref, qa_ref, m_ref, acc_ref, s_buf, m_buf, a_buf):
    pair = pl.program_id(1)
    n_tiles = q_ref.shape[1] // TQ

    row = lax.broadcasted_iota(jnp.int32, (LANES, VT_BLOCK), 0)

    def store(j, vt):
        one = jnp.ones_like(vt)
        vt_ref[0, j] = jnp.where(row < HEAD_DIM, vt, one)
        vt_ref[1, j] = jnp.where(row >= HEAD_DIM, vt, one)

    _transpose_values(v_ref, store)

    lane = lax.broadcasted_iota(jnp.int32, (TQ, LANES), 1)

    def query_operand(t, carry):
        rows = _rows(t, TQ)
        gate_q = gate_ref[0, rows, :]
        q = q_ref[0, rows, :]
        zero = jnp.zeros_like(q)
        for hh in range(HEADS_PER_PAIR):
            h = HEADS_PER_PAIR * pair + hh
            minus = jnp.zeros((TQ, LANES), F32)
            for i in range(GATE_PARTS):
                minus = jnp.where(lane == i * N_HEADS + h, -1.0, minus)
            gate_lanes = (_dot(gate_q, perm_ref[h]) + minus).astype(BF16)
            feat = (lane < HEAD_DIM) if hh == 0 else (lane >= HEAD_DIM)
            qa_ref[t, hh * TQ:(hh + 1) * TQ, :] = jnp.concatenate(
                [jnp.where(feat, q, zero), gate_lanes], axis=1)
        return carry

    lax.fori_loop(0, n_tiles, query_operand, 0)
    m_ref[...] = jnp.full(m_ref.shape, -jnp.inf, F32)
    acc_ref[...] = jnp.zeros(acc_ref.shape, F32)

    def scores(t, j, slot, masked):
        rows = _rows(j, TK)
        k_aug = jnp.concatenate([k_ref[0, rows, :], gate_ref[0, rows, :]], axis=1)
        s = _dot_nt(k_aug, qa_ref[t])
        if masked:
            s = jnp.where(_causal(strict=False), s, -jnp.inf)
        m_old = m_ref[t]
        m_new = jnp.maximum(m_old, jnp.max(s, axis=0, keepdims=True))
        m_ref[t] = m_new
        m_buf[slot] = m_new
        a_buf[slot] = jnp.exp2(m_old - m_new)
        s_buf[slot] = s

    def accumulate(t, j, slot):
        prob = jnp.exp2(s_buf[slot] - m_buf[slot]).astype(BF16)
        pv = jnp.concatenate(
            [_dot(_value_blocks(vt_ref, j, hh), prob[:, hh * TQ:(hh + 1) * TQ])
             for hh in range(HEADS_PER_PAIR)], axis=1)
        acc_ref[t] = acc_ref[t] * a_buf[slot] + pv

    _pipelined_sweep(n_tiles, (scores, accumulate), descending=False)

    def normalized(t):
        acc = acc_ref[t]
        denom = jnp.concatenate(
            [jnp.broadcast_to(acc[HEAD_DIM:HEAD_DIM + 1, :TQ], (LANES, TQ)),
             jnp.broadcast_to(acc[0:1, TQ:], (LANES, TQ))], axis=1)
        return acc / denom

    _store_outputs(o_ref, n_tiles, normalized)


def _sb_kernel(q_ref, k_ref, v_ref, tt_ref, o_ref,
               vt_ref, qs_ref, r_ref, acc_ref, z_buf, w_buf):
    n_tiles = q_ref.shape[1] // TQ

    def store(j, vt):
        vt_ref[j] = vt

    _transpose_values(v_ref, store)

    def query_operand(t, carry):
        qs_ref[t] = _stack_heads(q_ref[0, _rows(t, TQ), :])
        return carry

    lax.fori_loop(0, n_tiles, query_operand, 0)
    r_ref[...] = jnp.zeros(r_ref.shape, F32)
    acc_ref[...] = jnp.zeros(acc_ref.shape, F32)

    n_sub = TK // SB_BLOCK

    def logits(t, j, slot, masked):
        z = _dot_nt(k_ref[0, _rows(j, TK), :], qs_ref[t])
        if masked:
            z = jnp.where(_causal(strict=True), z, -jnp.inf)
        z_buf[slot] = z

    def weights(t, j, slot):
        r = r_ref[t]
        for g in reversed(range(n_sub)):
            rows = slice(g * SB_BLOCK, (g + 1) * SB_BLOCK)
            z = z_buf[slot, rows, :]
            sp = _softplus2(z)
            sums = _dot(tt_ref[...], sp.astype(BF16))
            w_buf[slot, rows, :] = jnp.exp2(z - sp - sums[:SB_BLOCK] - r).astype(BF16)
            r = r + sums[SB_BLOCK:SB_BLOCK + 1]
        r_ref[t] = r

    def accumulate(t, j, slot):
        acc_ref[t] += _dot(_value_blocks(vt_ref, j), w_buf[slot])

    _pipelined_sweep(n_tiles, (logits, weights, accumulate), descending=True)
    _store_outputs(o_ref, n_tiles, lambda t: acc_ref[t])


def _attention_call(body, name, qkv, extra_inputs, extra_specs, scratch_shapes):
    b, s, _ = qkv.shape

    def column_block(offset):
        return pl.BlockSpec((1, s, LANES), lambda b, p: (b, 0, offset + p))

    return pl.pallas_call(
        body,
        grid=(b, N_HEAD_PAIRS),
        in_specs=[column_block(0), column_block(N_HEAD_PAIRS), column_block(2 * N_HEAD_PAIRS)
                  ] + extra_specs,
        out_specs=column_block(0),
        out_shape=jax.ShapeDtypeStruct((b, s, D_MODEL), BF16),
        scratch_shapes=scratch_shapes,
        compiler_params=_params(("parallel", "parallel"), 56),
        name=name,
    )(qkv, qkv, qkv, *extra_inputs)


def _fox_attention(qkv, gate):
    b, s, _ = qkv.shape
    n_tiles = s // TQ
    lane = jnp.arange(LANES)
    perm = jnp.stack([
        sum(((lane[:, None] == i * N_HEADS + h) & (lane[None, :] == ONES_LANE + i))
            for i in range(GATE_PARTS))
        for h in range(N_HEADS)]).astype(BF16)
    cols = HEADS_PER_PAIR * TQ
    return _attention_call(
        _fox_kernel, "fox_attention", qkv, [gate, perm],
        [pl.BlockSpec((1, s, LANES), lambda b, p: (b, 0, 0)),
         pl.BlockSpec((N_HEADS, LANES, LANES), lambda b, p: (0, 0, 0))],
        [pltpu.VMEM((HEADS_PER_PAIR, s // VT_BLOCK, LANES, VT_BLOCK), BF16),
         pltpu.VMEM((n_tiles, cols, 2 * LANES), BF16),
         pltpu.VMEM((n_tiles, 1, cols), F32),
         pltpu.VMEM((n_tiles, LANES, cols), F32),
         pltpu.VMEM((2, TK, cols), F32),
         pltpu.VMEM((2, 1, cols), F32),
         pltpu.VMEM((2, 1, cols), F32)])


def _sb_attention(qkv):
    b, s, _ = qkv.shape
    n_tiles = s // TQ
    idx = jnp.arange(SB_BLOCK)
    later = (idx[None, :] > idx[:, None]).astype(BF16)
    tt = jnp.concatenate([later, jnp.ones((SUBLANES, SB_BLOCK), BF16)], axis=0)
    cols = HEADS_PER_PAIR * TQ
    return _attention_call(
        _sb_kernel, "sb_attention", qkv, [tt],
        [pl.BlockSpec((SB_BLOCK + SUBLANES, SB_BLOCK), lambda b, p: (0, 0))],
        [pltpu.VMEM((s // VT_BLOCK, LANES, VT_BLOCK), BF16),
         pltpu.VMEM((n_tiles, cols, LANES), BF16),
         pltpu.VMEM((n_tiles, 1, cols), F32),
         pltpu.VMEM((n_tiles, LANES, cols), F32),
         pltpu.VMEM((2, TK, cols), F32),
         pltpu.VMEM((2, TK, cols), BF16)])


def _mm_resid_norm_kernel(a_ref, w_ref, h_ref, g_ref, hs_ref, xn_ref):
    h = _dot(a_ref[...], w_ref[...]) + h_ref[...]
    hs_ref[...] = h
    xn_ref[...] = _rms(h, g_ref[...]).astype(xn_ref.dtype)


def _mm_resid_final_kernel(a_ref, w_ref, h_ref, g_ref, o_ref):
    h = _dot(a_ref[...], w_ref[...]) + h_ref[...]
    o_ref[...] = _rms(h, g_ref[...]).astype(o_ref.dtype)


def _mm_resid_norm(a, w, hs, g, final=False):
    m, k = a.shape
    d = w.shape[1]
    row = lambda i: (i, 0)
    in_specs = [pl.BlockSpec((ROW_TILE, k), row),
                pl.BlockSpec((k, d), lambda i: (0, 0)),
                pl.BlockSpec((ROW_TILE, d), row),
                pl.BlockSpec((1, d), lambda i: (0, 0))]
    if final:
        body = _mm_resid_final_kernel
        out_specs = pl.BlockSpec((ROW_TILE, d), row)
        out_shape = jax.ShapeDtypeStruct((m, d), F32)
    else:
        body = _mm_resid_norm_kernel
        out_specs = [pl.BlockSpec((ROW_TILE, d), row), pl.BlockSpec((ROW_TILE, d), row)]
        out_shape = [jax.ShapeDtypeStruct((m, d), F32), jax.ShapeDtypeStruct((m, d), BF16)]
    return pl.pallas_call(
        body,
        grid=(m // ROW_TILE,),
        in_specs=in_specs,
        out_specs=out_specs,
        out_shape=out_shape,
        compiler_params=_params(("parallel",), 48),
        name="matmul_residual_norm",
    )(a, w, hs, g.reshape(1, d))


def _ffn_up_kernel(x_ref, w_ref, wc_ref, bc_ref, o_ref, carry_ref, *, tiles_per_seq):
    i = pl.program_id(0)

    @pl.when(i % tiles_per_seq == 0)
    def _():
        carry_ref[...] = jnp.zeros(carry_ref.shape, F32)

    x = x_ref[...]
    tm = x.shape[0]
    row = lax.broadcasted_iota(jnp.int32, (SUBLANES, 2 * FF_CHUNK), 0)

    def shifted(h, prev, shift):
        rolled = pltpu.roll(h, shift, 0)
        head = rolled[:SUBLANES]
        for i in range(shift):
            head = jnp.where(row == i, prev[SUBLANES - shift + i:SUBLANES - shift + i + 1], head)
        return jnp.concatenate([head, rolled[SUBLANES:]], axis=0)

    for c in range(N_FF_CHUNKS):
        h = _dot(x, w_ref[c])
        prev = carry_ref[c]
        h1 = shifted(h, prev, 1)
        h2 = shifted(h, prev, 2)
        wc = wc_ref[c]
        hc = bc_ref[c] + wc[0:1] * h2 + wc[1:2] * h1 + wc[2:3] * h
        u = hc[:, :FF_CHUNK]
        g = hc[:, FF_CHUNK:]
        act = g * u / (1.0 + jnp.exp(-g))
        o_ref[:, c * FF_CHUNK:(c + 1) * FF_CHUNK] = act.astype(o_ref.dtype)
        carry_ref[c] = h[tm - SUBLANES:tm]


def _ffn_up(xn, w_chunks, wc_chunks, bc_chunks, seq_len):
    m, k = xn.shape
    kern = functools.partial(_ffn_up_kernel, tiles_per_seq=seq_len // ROW_TILE)
    return pl.pallas_call(
        kern,
        grid=(m // ROW_TILE,),
        in_specs=[pl.BlockSpec((ROW_TILE, k), lambda i: (i, 0)),
                  pl.BlockSpec((N_FF_CHUNKS, k, 2 * FF_CHUNK), lambda i: (0, 0, 0)),
                  pl.BlockSpec((N_FF_CHUNKS, CONV_WIDTH, 2 * FF_CHUNK), lambda i: (0, 0, 0)),
                  pl.BlockSpec((N_FF_CHUNKS, 1, 2 * FF_CHUNK), lambda i: (0, 0, 0))],
        out_specs=pl.BlockSpec((ROW_TILE, D_FF), lambda i: (i, 0)),
        out_shape=jax.ShapeDtypeStruct((m, D_FF), BF16),
        scratch_shapes=[pltpu.VMEM((N_FF_CHUNKS, SUBLANES, 2 * FF_CHUNK), F32)],
        compiler_params=_params(("arbitrary",), 56),
        name="ffn_up_conv_gate",
    )(xn, w_chunks, wc_chunks, bc_chunks)


def _chunk_columns(t):
    r = t.shape[0]
    t = t.reshape(r, 2, N_FF_CHUNKS, FF_CHUNK)
    return t.transpose(2, 0, 1, 3).reshape(N_FF_CHUNKS, r, 2 * FF_CHUNK)


def kernel(x, attn_norm, ffn_norm, final_norm, fox_w_qkvf, fox_b_f, fox_w_o,
           sb_w_qkv, sb_w_o, ffn_w_up, ffn_w_conv, ffn_b_conv, ffn_w_down):
    b, s, d = x.shape
    m = b * s
    assert d == D_MODEL and s % TQ == 0 and s % ROW_TILE == 0 and s % VT_GROUP == 0 and TQ == TK

    colscale = jnp.concatenate([jnp.full((1, d), ATTN_SCALE * LOG2E, F32),
                                jnp.ones((1, 2 * d), F32)], axis=1)

    hs = x.reshape(m, d)
    xn = _rmsnorm(hs, attn_norm[0], BF16)
    out = None
    for i in range(DEPTH):
        j = i // 2
        if i % 2 == 0:
            w = fox_w_qkvf[j]
            w_gate = jnp.pad(w[:, 3 * d:], ((0, 0), (0, LANES - N_HEADS))).astype(BF16)
            qkv, f_logit = _proj(xn, w[:, :3 * d].astype(BF16), colscale, w_gate)
            b_gate = jnp.pad(fox_b_f[j], (0, LANES - N_HEADS)).reshape(1, LANES)
            gate = _gate_operand(f_logit.reshape(b, s, LANES), b_gate)
            o = _fox_attention(qkv.reshape(b, s, 3 * d), gate)
            w_o = fox_w_o[j]
        else:
            qkv = _proj(xn, sb_w_qkv[j].astype(BF16), colscale)[0]
            o = _sb_attention(qkv.reshape(b, s, 3 * d))
            w_o = sb_w_o[j]
        hs, xn = _mm_resid_norm(o.reshape(m, d), w_o.astype(BF16), hs, ffn_norm[i])
        act = _ffn_up(xn,
                      _chunk_columns(ffn_w_up[i]).astype(BF16),
                      _chunk_columns(ffn_w_conv[i]),
                      _chunk_columns(ffn_b_conv[i].reshape(1, -1)),
                      s)
        w_down = ffn_w_down[i].astype(BF16)
        if i + 1 < DEPTH:
            hs, xn = _mm_resid_norm(act, w_down, hs, attn_norm[i + 1])
        else:
            out = _mm_resid_norm(act, w_down, hs, final_norm, final=True)
    return out.reshape(b, s, d)
```

```python
import functools
import math

import jax
import jax.numpy as jnp
from jax import lax
from jax.experimental import pallas as pl
from jax.experimental.pallas import tpu as pltpu

D_MODEL = 1024
N_HEADS = 16
HEAD_DIM = D_MODEL // N_HEADS
D_FF = 2816
CONV_WIDTH = 3
DEPTH = 4
NORM_EPS = 1e-6
ATTN_SCALE = HEAD_DIM ** -0.5
LOG2E = math.log2(math.e)

F32 = jnp.float32
BF16 = jnp.bfloat16

LANES = 128
SUBLANES = 8
HEADS_PER_PAIR = LANES // HEAD_DIM
N_HEAD_PAIRS = N_HEADS // HEADS_PER_PAIR
ROW_TILE = 512
PROJ_CHUNK = 512
FF_CHUNK = 256
N_FF_CHUNKS = D_FF // FF_CHUNK
FF_ROWS = 128
TQ = 512
TK = 512
VT_BLOCK = 256
VT_GROUP = 1024
PIPELINE_UNROLL = 4
CUM_BLOCK = 128
SB_BLOCK = 256
GATE_PARTS = 3
ONES_LANE = GATE_PARTS * N_HEADS
MIB = 2 ** 20


def _params(semantics, vmem_mib):
    return pltpu.CompilerParams(dimension_semantics=semantics,
                                vmem_limit_bytes=vmem_mib * MIB)


def _dot(a, b):
    return jnp.dot(a, b, preferred_element_type=F32)


def _dot_nt(a, b):
    return lax.dot_general(a, b, (((1,), (1,)), ((), ())), preferred_element_type=F32)


def _split_bf16(x, parts):
    pieces = []
    for _ in range(parts - 1):
        p = x.astype(BF16)
        pieces.append(p)
        x = x - p.astype(F32)
    pieces.append(x.astype(BF16))
    return pieces


def _rms(h, g):
    ms = jnp.mean(h * h, axis=-1, keepdims=True)
    return h * lax.rsqrt(ms + NORM_EPS) * g


def _rmsnorm_kernel(x_ref, g_ref, o_ref):
    o_ref[...] = _rms(x_ref[...], g_ref[...]).astype(o_ref.dtype)


def _rmsnorm(x, g, out_dtype):
    m, d = x.shape
    return pl.pallas_call(
        _rmsnorm_kernel,
        grid=(m // ROW_TILE,),
        in_specs=[pl.BlockSpec((ROW_TILE, d), lambda i: (i, 0)),
                  pl.BlockSpec((1, d), lambda i: (0, 0))],
        out_specs=pl.BlockSpec((ROW_TILE, d), lambda i: (i, 0)),
        out_shape=jax.ShapeDtypeStruct((m, d), out_dtype),
        compiler_params=_params(("parallel",), 32),
        name="rmsnorm",
    )(x, g.reshape(1, d))


def _proj_kernel(x_ref, w_ref, cs_ref, o_ref):
    x = x_ref[...]
    n = w_ref.shape[1]
    for c in range(n // PROJ_CHUNK):
        sl = slice(c * PROJ_CHUNK, (c + 1) * PROJ_CHUNK)
        o_ref[:, sl] = (_dot(x, w_ref[:, sl]) * cs_ref[:, sl]).astype(o_ref.dtype)


def _proj_gate_kernel(x_ref, w_ref, cs_ref, wf_ref, o_ref, f_ref):
    _proj_kernel(x_ref, w_ref, cs_ref, o_ref)
    f_ref[...] = _dot(x_ref[...], wf_ref[...])


def _proj(xn, w, colscale, w_gate=None):
    m, k = xn.shape
    n = w.shape[1]
    in_specs = [pl.BlockSpec((ROW_TILE, k), lambda i: (i, 0)),
                pl.BlockSpec((k, n), lambda i: (0, 0)),
                pl.BlockSpec((1, n), lambda i: (0, 0))]
    out_specs = [pl.BlockSpec((ROW_TILE, n), lambda i: (i, 0))]
    out_shape = [jax.ShapeDtypeStruct((m, n), BF16)]
    args = [xn, w, colscale]
    body = _proj_kernel
    if w_gate is not None:
        in_specs.append(pl.BlockSpec((k, LANES), lambda i: (0, 0)))
        out_specs.append(pl.BlockSpec((ROW_TILE, LANES), lambda i: (i, 0)))
        out_shape.append(jax.ShapeDtypeStruct((m, LANES), F32))
        args.append(w_gate)
        body = _proj_gate_kernel
    return pl.pallas_call(
        body,
        grid=(m // ROW_TILE,),
        in_specs=in_specs,
        out_specs=out_specs,
        out_shape=out_shape,
        compiler_params=_params(("parallel",), 48),
        name="qkv_proj",
    )(*args)


def _softplus2(y):
    neg_abs = pltpu.bitcast(pltpu.bitcast(y, jnp.uint32) | jnp.uint32(0x80000000), F32)
    return jnp.maximum(y, 0.0) + jnp.log(1.0 + jnp.exp2(neg_abs)) * LOG2E


def _gate_cumsum_kernel(f_ref, b_ref, tri_ref, place_ref, o_ref):
    n_blocks = f_ref.shape[1] // CUM_BLOCK
    tri = tri_ref[...]
    lane = lax.broadcasted_iota(jnp.int32, (CUM_BLOCK, LANES), 1)
    ones = jnp.where((lane >= ONES_LANE) & (lane < ONES_LANE + GATE_PARTS), 1.0, 0.0)

    def body(r, carry):
        rows = pl.ds(pl.multiple_of(r * CUM_BLOCK, CUM_BLOCK), CUM_BLOCK)
        logit2 = (f_ref[0, rows, :] + b_ref[...]) * LOG2E
        ls = -_softplus2(-logit2)
        c2 = carry
        for piece in _split_bf16(ls, GATE_PARTS):
            c2 = c2 + _dot(tri, piece)
        operand = ones
        for i, piece in enumerate(_split_bf16(c2, GATE_PARTS)):
            operand = operand + _dot(piece, place_ref[i])
        o_ref[0, rows, :] = operand.astype(o_ref.dtype)
        return c2[CUM_BLOCK - 1:CUM_BLOCK, :]

    lax.fori_loop(0, n_blocks, body, jnp.zeros((1, LANES), F32))


def _gate_operand(f_logit, b_gate):
    b, s, _ = f_logit.shape
    idx = jnp.arange(CUM_BLOCK)
    tri = (idx[None, :] <= idx[:, None]).astype(BF16)
    lane = jnp.arange(LANES)
    place = jnp.stack([((lane[:, None] < N_HEADS) & (lane[None, :] == i * N_HEADS + lane[:, None]))
                       for i in range(GATE_PARTS)]).astype(BF16)
    return pl.pallas_call(
        _gate_cumsum_kernel,
        grid=(b,),
        in_specs=[pl.BlockSpec((1, s, LANES), lambda i: (i, 0, 0)),
                  pl.BlockSpec((1, LANES), lambda i: (0, 0)),
                  pl.BlockSpec((CUM_BLOCK, CUM_BLOCK), lambda i: (0, 0)),
                  pl.BlockSpec((GATE_PARTS, LANES, LANES), lambda i: (0, 0, 0))],
        out_specs=pl.BlockSpec((1, s, LANES), lambda i: (i, 0, 0)),
        out_shape=jax.ShapeDtypeStruct((b, s, LANES), BF16),
        compiler_params=_params(("parallel",), 32),
        name="gate_cumsum",
    )(f_logit, b_gate, tri, place)


def _stack_heads(q):
    lane = lax.broadcasted_iota(jnp.int32, q.shape, 1)
    zero = jnp.zeros_like(q)
    return jnp.concatenate([jnp.where(lane < HEAD_DIM, q, zero),
                            jnp.where(lane >= HEAD_DIM, q, zero)], axis=0)


def _causal(strict):
    shape = (TK, HEADS_PER_PAIR * TQ)
    key = lax.broadcasted_iota(jnp.int32, shape, 0)
    query = lax.broadcasted_iota(jnp.int32, shape, 1) & (TQ - 1)
    return key < query if strict else key <= query


def _rows(index, size):
    if isinstance(index, int):
        return pl.ds(index * size, size)
    return pl.ds(pl.multiple_of(index * size, size), size)


def _transpose_values(v_ref, store):
    row = lax.broadcasted_iota(jnp.int32, (LANES, LANES), 0)
    col = lax.broadcasted_iota(jnp.int32, (LANES, LANES), 1)
    eye = (row == col).astype(BF16)

    group = VT_GROUP // VT_BLOCK

    def body(i, carry):
        vt = _dot_nt(eye, v_ref[0, _rows(i, VT_GROUP), :]).astype(BF16)
        for g in range(group):
            store(i * group + g, vt[:, g * VT_BLOCK:(g + 1) * VT_BLOCK])
        return carry

    lax.fori_loop(0, v_ref.shape[1] // VT_GROUP, body, 0)


def _value_blocks(vt_ref, j, *lead):
    per_block = TK // VT_BLOCK
    return jnp.concatenate([vt_ref[(*lead, j * per_block + i)] for i in range(per_block)], axis=1)


def _pipelined_sweep(n_tiles, stages, descending):
    depth = len(stages)
    pairs = [(t, j) for t in range(n_tiles)
             for j in (range(t, -1, -1) if descending else range(t + 1))]
    n = len(pairs)
    assert n >= depth - 1 + PIPELINE_UNROLL and PIPELINE_UNROLL % 2 == 0

    def run(step_pairs, parity, masked, active):
        for s in active:
            slot = parity if s % 2 == 0 else 1 - parity
            tile, block = step_pairs[s]
            if s == 0:
                stages[0](tile, block, slot, masked)
            else:
                stages[s](tile, block, slot)

    def static_step(u):
        active = [s for s in range(depth) if 0 <= u - s < n]
        step_pairs = {s: pairs[u - s] for s in active}
        masked = 0 in step_pairs and step_pairs[0][0] == step_pairs[0][1]
        run(step_pairs, u & 1, masked, active)

    def advance(tile, block):
        if descending:
            last = block == 0
            return jnp.where(last, tile + 1, tile), jnp.where(last, tile + 1, block - 1)
        last = block == tile
        return jnp.where(last, tile + 1, tile), jnp.where(last, 0, block + 1)

    start = depth - 1 + (n - depth + 1) % PIPELINE_UNROLL
    for u in range(start):
        static_step(u)

    def on_diagonal(pair):
        return pair[0] == pair[1]

    mask_cases = sorted({tuple(on_diagonal(pairs[u + i]) for i in range(PIPELINE_UNROLL))
                         for u in range(start, n, PIPELINE_UNROLL)})

    def body(_, carry):
        window = [(carry[2 * s], carry[2 * s + 1]) for s in range(depth)]
        for _ in range(PIPELINE_UNROLL - 1):
            window.insert(0, advance(*window[0]))
        heads = window[:PIPELINE_UNROLL][::-1]
        for masked in mask_cases:
            cond = on_diagonal(heads[0]) == masked[0]
            for i in range(1, PIPELINE_UNROLL):
                cond = jnp.logical_and(cond, on_diagonal(heads[i]) == masked[i])

            @pl.when(cond)
            def _():
                for i in range(PIPELINE_UNROLL):
                    offset = PIPELINE_UNROLL - 1 - i
                    step_pairs = {s: window[offset + s] for s in range(depth)}
                    run(step_pairs, (start + i) & 1, masked[i], range(depth))
        window.insert(0, advance(*window[0]))
        return tuple(v for pair in window[:depth] for v in pair)

    init = ()
    for s in range(depth):
        init += tuple(jnp.int32(v) for v in pairs[start - s])
    lax.fori_loop(0, (n - start) // PIPELINE_UNROLL, body, init)

    for u in range(n, n + depth - 1):
        static_step(u)


def _store_outputs(o_ref, n_tiles, normalized):
    def body(t, carry):
        acc_t = normalized(t)
        out_t = jnp.concatenate([acc_t[:HEAD_DIM, :TQ], acc_t[HEAD_DIM:, TQ:]], axis=0)
        o_ref[0, _rows(t, TQ), :] = out_t.T.astype(o_ref.dtype)
        return carry

    lax.fori_loop(0, n_tiles, body, 0)


def _fox_kernel(q_ref, k_ref, v_ref, gate_ref, perm_ref, o_ref,
                vt_ref, qa_ref, m_ref, acc_ref, s_buf, m_buf, a_buf):
    pair = pl.program_id(1)
    n_tiles = q_ref.shape[1] // TQ

    row = lax.broadcasted_iota(jnp.int32, (LANES, VT_BLOCK), 0)

    def store(j, vt):
        one = jnp.ones_like(vt)
        vt_ref[0, j] = jnp.where(row < HEAD_DIM, vt, one)
        vt_ref[1, j] = jnp.where(row >= HEAD_DIM, vt, one)

    _transpose_values(v_ref, store)

    lane = lax.broadcasted_iota(jnp.int32, (TQ, LANES), 1)

    def query_operand(t, carry):
        rows = _rows(t, TQ)
        gate_q = gate_ref[0, rows, :]
        q = q_ref[0, rows, :]
        zero = jnp.zeros_like(q)
        for hh in range(HEADS_PER_PAIR):
            h = HEADS_PER_PAIR * pair + hh
            minus = jnp.zeros((TQ, LANES), F32)
            for i in range(GATE_PARTS):
                minus = jnp.where(lane == i * N_HEADS + h, -1.0, minus)
            gate_lanes = (_dot(gate_q, perm_ref[h]) + minus).astype(BF16)
            feat = (lane < HEAD_DIM) if hh == 0 else (lane >= HEAD_DIM)
            qa_ref[t, hh * TQ:(hh + 1) * TQ, :] = jnp.concatenate(
                [jnp.where(feat, q, zero), gate_lanes], axis=1)
        return carry

    lax.fori_loop(0, n_tiles, query_operand, 0)
    m_ref[...] = jnp.full(m_ref.shape, -jnp.inf, F32)
    acc_ref[...] = jnp.zeros(acc_ref.shape, F32)

    def scores(t, j, slot, masked):
        rows = _rows(j, TK)
        k_aug = jnp.concatenate([k_ref[0, rows, :], gate_ref[0, rows, :]], axis=1)
        s = _dot_nt(k_aug, qa_ref[t])
        if masked:
            s = jnp.where(_causal(strict=False), s, -jnp.inf)
        m_old = m_ref[t]
        m_new = jnp.maximum(m_old, jnp.max(s, axis=0, keepdims=True))
        m_ref[t] = m_new
        m_buf[slot] = m_new
        a_buf[slot] = jnp.exp2(m_old - m_new)
        s_buf[slot] = s

    def accumulate(t, j, slot):
        prob = jnp.exp2(s_buf[slot] - m_buf[slot]).astype(BF16)
        pv = jnp.concatenate(
            [_dot(_value_blocks(vt_ref, j, hh), prob[:, hh * TQ:(hh + 1) * TQ])
             for hh in range(HEADS_PER_PAIR)], axis=1)
        acc_ref[t] = acc_ref[t] * a_buf[slot] + pv

    _pipelined_sweep(n_tiles, (scores, accumulate), descending=False)

    def normalized(t):
        acc = acc_ref[t]
        denom = jnp.concatenate(
            [jnp.broadcast_to(acc[HEAD_DIM:HEAD_DIM + 1, :TQ], (LANES, TQ)),
             jnp.broadcast_to(acc[0:1, TQ:], (LANES, TQ))], axis=1)
        return acc / denom

    _store_outputs(o_ref, n_tiles, normalized)


def _sb_kernel(q_ref, k_ref, v_ref, tt_ref, o_ref,
               vt_ref, qs_ref, r_ref, acc_ref, z_buf, w_buf):
    n_tiles = q_ref.shape[1] // TQ

    def store(j, vt):
        vt_ref[j] = vt

    _transpose_values(v_ref, store)

    def query_operand(t, carry):
        qs_ref[t] = _stack_heads(q_ref[0, _rows(t, TQ), :])
        return carry

    lax.fori_loop(0, n_tiles, query_operand, 0)
    r_ref[...] = jnp.zeros(r_ref.shape, F32)
    acc_ref[...] = jnp.zeros(acc_ref.shape, F32)

    n_sub = TK // SB_BLOCK

    def logits(t, j, slot, masked):
        z = _dot_nt(k_ref[0, _rows(j, TK), :], qs_ref[t])
        if masked:
            z = jnp.where(_causal(strict=True), z, -jnp.inf)
        z_buf[slot] = z

    def weights(t, j, slot):
        r = r_ref[t]
        for g in reversed(range(n_sub)):
            rows = slice(g * SB_BLOCK, (g + 1) * SB_BLOCK)
            z = z_buf[slot, rows, :]
            sp = _softplus2(z)
            sums = _dot(tt_ref[...], sp.astype(BF16))
            w_buf[slot, rows, :] = jnp.exp2(z - sp - sums[:SB_BLOCK] - r).astype(BF16)
            r = r + sums[SB_BLOCK:SB_BLOCK + 1]
        r_ref[t] = r

    def accumulate(t, j, slot):
        acc_ref[t] += _dot(_value_blocks(vt_ref, j), w_buf[slot])

    _pipelined_sweep(n_tiles, (logits, weights, accumulate), descending=True)
    _store_outputs(o_ref, n_tiles, lambda t: acc_ref[t])


def _attention_call(body, name, qkv, extra_inputs, extra_specs, scratch_shapes):
    b, s, _ = qkv.shape

    def column_block(offset):
        return pl.BlockSpec((1, s, LANES), lambda b, p: (b, 0, offset + p))

    return pl.pallas_call(
        body,
        grid=(b, N_HEAD_PAIRS),
        in_specs=[column_block(0), column_block(N_HEAD_PAIRS), column_block(2 * N_HEAD_PAIRS)
                  ] + extra_specs,
        out_specs=column_block(0),
        out_shape=jax.ShapeDtypeStruct((b, s, D_MODEL), BF16),
        scratch_shapes=scratch_shapes,
        compiler_params=_params(("parallel", "parallel"), 56),
        name=name,
    )(qkv, qkv, qkv, *extra_inputs)


def _fox_attention(qkv, gate):
    b, s, _ = qkv.shape
    n_tiles = s // TQ
    lane = jnp.arange(LANES)
    perm = jnp.stack([
        sum(((lane[:, None] == i * N_HEADS + h) & (lane[None, :] == ONES_LANE + i))
            for i in range(GATE_PARTS))
        for h in range(N_HEADS)]).astype(BF16)
    cols = HEADS_PER_PAIR * TQ
    return _attention_call(
        _fox_kernel, "fox_attention", qkv, [gate, perm],
        [pl.BlockSpec((1, s, LANES), lambda b, p: (b, 0, 0)),
         pl.BlockSpec((N_HEADS, LANES, LANES), lambda b, p: (0, 0, 0))],
        [pltpu.VMEM((HEADS_PER_PAIR, s // VT_BLOCK, LANES, VT_BLOCK), BF16),
         pltpu.VMEM((n_tiles, cols, 2 * LANES), BF16),
         pltpu.VMEM((n_tiles, 1, cols), F32),
         pltpu.VMEM((n_tiles, LANES, cols), F32),
         pltpu.VMEM((2, TK, cols), F32),
         pltpu.VMEM((2, 1, cols), F32),
         pltpu.VMEM((2, 1, cols), F32)])


def _sb_attention(qkv):
    b, s, _ = qkv.shape
    n_tiles = s // TQ
    idx = jnp.arange(SB_BLOCK)
    later = (idx[None, :] > idx[:, None]).astype(BF16)
    tt = jnp.concatenate([later, jnp.ones((SUBLANES, SB_BLOCK), BF16)], axis=0)
    cols = HEADS_PER_PAIR * TQ
    return _attention_call(
        _sb_kernel, "sb_attention", qkv, [tt],
        [pl.BlockSpec((SB_BLOCK + SUBLANES, SB_BLOCK), lambda b, p: (0, 0))],
        [pltpu.VMEM((s // VT_BLOCK, LANES, VT_BLOCK), BF16),
         pltpu.VMEM((n_tiles, cols, LANES), BF16),
         pltpu.VMEM((n_tiles, 1, cols), F32),
         pltpu.VMEM((n_tiles, LANES, cols), F32),
         pltpu.VMEM((2, TK, cols), F32),
         pltpu.VMEM((2, TK, cols), BF16)])


def _mm_resid_norm_kernel(a_ref, w_ref, h_ref, g_ref, hs_ref, xn_ref):
    h = _dot(a_ref[...], w_ref[...]) + h_ref[...]
    hs_ref[...] = h
    xn_ref[...] = _rms(h, g_ref[...]).astype(xn_ref.dtype)


def _mm_resid_final_kernel(a_ref, w_ref, h_ref, g_ref, o_ref):
    h = _dot(a_ref[...], w_ref[...]) + h_ref[...]
    o_ref[...] = _rms(h, g_ref[...]).astype(o_ref.dtype)


def _mm_resid_norm(a, w, hs, g, final=False):
    m, k = a.shape
    d = w.shape[1]
    row = lambda i: (i, 0)
    in_specs = [pl.BlockSpec((ROW_TILE, k), row),
                pl.BlockSpec((k, d), lambda i: (0, 0)),
                pl.BlockSpec((ROW_TILE, d), row),
                pl.BlockSpec((1, d), lambda i: (0, 0))]
    if final:
        body = _mm_resid_final_kernel
        out_specs = pl.BlockSpec((ROW_TILE, d), row)
        out_shape = jax.ShapeDtypeStruct((m, d), F32)
    else:
        body = _mm_resid_norm_kernel
        out_specs = [pl.BlockSpec((ROW_TILE, d), row), pl.BlockSpec((ROW_TILE, d), row)]
        out_shape = [jax.ShapeDtypeStruct((m, d), F32), jax.ShapeDtypeStruct((m, d), BF16)]
    return pl.pallas_call(
        body,
        grid=(m // ROW_TILE,),
        in_specs=in_specs,
        out_specs=out_specs,
        out_shape=out_shape,
        compiler_params=_params(("parallel",), 48),
        name="matmul_residual_norm",
    )(a, w, hs, g.reshape(1, d))


def _ffn_up_kernel(x_ref, w_ref, wc_ref, bc_ref, o_ref, carry_ref, *, tiles_per_seq):
    i = pl.program_id(0)

    @pl.when(i % tiles_per_seq == 0)
    def _():
        carry_ref[...] = jnp.zeros(carry_ref.shape, F32)

    row = lax.broadcasted_iota(jnp.int32, (SUBLANES, 2 * FF_CHUNK), 0)

    def shifted(h, prev, shift):
        rolled = pltpu.roll(h, shift, 0)
        head = rolled[:SUBLANES]
        for i in range(shift):
            head = jnp.where(row == i, prev[SUBLANES - shift + i:SUBLANES - shift + i + 1], head)
        return jnp.concatenate([head, rolled[SUBLANES:]], axis=0)

    for r in range(x_ref.shape[0] // FF_ROWS):
        rows = slice(r * FF_ROWS, (r + 1) * FF_ROWS)
        x = x_ref[rows, :]
        for c in range(N_FF_CHUNKS):
            h = _dot(x, w_ref[c])
            prev = carry_ref[c]
            h1 = shifted(h, prev, 1)
            h2 = shifted(h, prev, 2)
            wc = wc_ref[c]
            hc = bc_ref[c] + wc[0:1] * h2 + wc[1:2] * h1 + wc[2:3] * h
            u = hc[:, :FF_CHUNK]
            g = hc[:, FF_CHUNK:]
            act = g * u / (1.0 + jnp.exp(-g))
            o_ref[rows, c * FF_CHUNK:(c + 1) * FF_CHUNK] = act.astype(o_ref.dtype)
            carry_ref[c] = h[FF_ROWS - SUBLANES:FF_ROWS]


def _ffn_up(xn, w_chunks, wc_chunks, bc_chunks, seq_len):
    m, k = xn.shape
    kern = functools.partial(_ffn_up_kernel, tiles_per_seq=seq_len // ROW_TILE)
    return pl.pallas_call(
        kern,
        grid=(m // ROW_TILE,),
        in_specs=[pl.BlockSpec((ROW_TILE, k), lambda i: (i, 0)),
                  pl.BlockSpec((N_FF_CHUNKS, k, 2 * FF_CHUNK), lambda i: (0, 0, 0)),
                  pl.BlockSpec((N_FF_CHUNKS, CONV_WIDTH, 2 * FF_CHUNK), lambda i: (0, 0, 0)),
                  pl.BlockSpec((N_FF_CHUNKS, 1, 2 * FF_CHUNK), lambda i: (0, 0, 0))],
        out_specs=pl.BlockSpec((ROW_TILE, D_FF), lambda i: (i, 0)),
        out_shape=jax.ShapeDtypeStruct((m, D_FF), BF16),
        scratch_shapes=[pltpu.VMEM((N_FF_CHUNKS, SUBLANES, 2 * FF_CHUNK), F32)],
        compiler_params=_params(("arbitrary",), 56),
        name="ffn_up_conv_gate",
    )(xn, w_chunks, wc_chunks, bc_chunks)


def _chunk_columns(t):
    r = t.shape[0]
    t = t.reshape(r, 2, N_FF_CHUNKS, FF_CHUNK)
    return t.transpose(2, 0, 1, 3).reshape(N_FF_CHUNKS, r, 2 * FF_CHUNK)


def kernel(x, attn_norm, ffn_norm, final_norm, fox_w_qkvf, fox_b_f, fox_w_o,
           sb_w_qkv, sb_w_o, ffn_w_up, ffn_w_conv, ffn_b_conv, ffn_w_down):
    b, s, d = x.shape
    m = b * s
    assert d == D_MODEL and s % TQ == 0 and s % ROW_TILE == 0 and s % VT_GROUP == 0 and TQ == TK

    colscale = jnp.concatenate([jnp.full((1, d), ATTN_SCALE * LOG2E, F32),
                                jnp.ones((1, 2 * d), F32)], axis=1)

    hs = x.reshape(m, d)
    xn = _rmsnorm(hs, attn_norm[0], BF16)
    out = None
    for i in range(DEPTH):
        j = i // 2
        if i % 2 == 0:
            w = fox_w_qkvf[j]
            w_gate = jnp.pad(w[:, 3 * d:], ((0, 0), (0, LANES - N_HEADS))).astype(BF16)
            qkv, f_logit = _proj(xn, w[:, :3 * d].astype(BF16), colscale, w_gate)
            b_gate = jnp.pad(fox_b_f[j], (0, LANES - N_HEADS)).reshape(1, LANES)
            gate = _gate_operand(f_logit.reshape(b, s, LANES), b_gate)
            o = _fox_attention(qkv.reshape(b, s, 3 * d), gate)
            w_o = fox_w_o[j]
        else:
            qkv = _proj(xn, sb_w_qkv[j].astype(BF16), colscale)[0]
            o = _sb_attention(qkv.reshape(b, s, 3 * d))
            w_o = sb_w_o[j]
        hs, xn = _mm_resid_norm(o.reshape(m, d), w_o.astype(BF16), hs, ffn_norm[i])
        act = _ffn_up(xn,
                      _chunk_columns(ffn_w_up[i]).astype(BF16),
                      _chunk_columns(ffn_w_conv[i]),
                      _chunk_columns(ffn_b_conv[i].reshape(1, -1)),
                      s)
        w_down = ffn_w_down[i].astype(BF16)
        if i + 1 < DEPTH:
            hs, xn = _mm_resid_norm(act, w_down, hs, attn_norm[i + 1])
        else:
            out = _mm_resid_norm(act, w_down, hs, final_norm, final=True)
    return out.reshape(b, s, d)
```

```python
import functools
import math

import jax
import jax.numpy as jnp
from jax import lax
from jax.experimental import pallas as pl
from jax.experimental.pallas import tpu as pltpu

D_MODEL = 1024
N_HEADS = 16
HEAD_DIM = D_MODEL // N_HEADS
D_FF = 2816
CONV_WIDTH = 3
DEPTH = 4
NORM_EPS = 1e-6
ATTN_SCALE = HEAD_DIM ** -0.5
LOG2E = math.log2(math.e)

F32 = jnp.float32
BF16 = jnp.bfloat16

LANES = 128
SUBLANES = 8
HEADS_PER_PAIR = LANES // HEAD_DIM
N_HEAD_PAIRS = N_HEADS // HEADS_PER_PAIR
ROW_TILE = 512
PROJ_CHUNK = 512
FF_CHUNK = 256
N_FF_CHUNKS = D_FF // FF_CHUNK
FF_ROWS = 128
TQ = 512
TK = 512
VT_BLOCK = 256
VT_GROUP = 1024
PIPELINE_UNROLL = 4
CUM_BLOCK = 128
SB_BLOCK = 256
GATE_PARTS = 3
ONES_LANE = GATE_PARTS * N_HEADS
MIB = 2 ** 20


def _params(semantics, vmem_mib):
    return pltpu.CompilerParams(dimension_semantics=semantics,
                                vmem_limit_bytes=vmem_mib * MIB)


def _dot(a, b):
    return jnp.dot(a, b, preferred_element_type=F32)


def _dot_nt(a, b):
    return lax.dot_general(a, b, (((1,), (1,)), ((), ())), preferred_element_type=F32)


def _split_bf16(x, parts):
    pieces = []
    for _ in range(parts - 1):
        p = x.astype(BF16)
        pieces.append(p)
        x = x - p.astype(F32)
    pieces.append(x.astype(BF16))
    return pieces


def _rms(h, g):
    ms = jnp.mean(h * h, axis=-1, keepdims=True)
    return h * lax.rsqrt(ms + NORM_EPS) * g


def _rmsnorm_kernel(x_ref, g_ref, o_ref):
    o_ref[...] = _rms(x_ref[...], g_ref[...]).astype(o_ref.dtype)


def _rmsnorm(x, g, out_dtype):
    m, d = x.shape
    return pl.pallas_call(
        _rmsnorm_kernel,
        grid=(m // ROW_TILE,),
        in_specs=[pl.BlockSpec((ROW_TILE, d), lambda i: (i, 0)),
                  pl.BlockSpec((1, d), lambda i: (0, 0))],
        out_specs=pl.BlockSpec((ROW_TILE, d), lambda i: (i, 0)),
        out_shape=jax.ShapeDtypeStruct((m, d), out_dtype),
        compiler_params=_params(("parallel",), 32),
        name="rmsnorm",
    )(x, g.reshape(1, d))


def _proj_kernel(x_ref, w_ref, cs_ref, o_ref):
    x = x_ref[...]
    n = w_ref.shape[1]
    for c in range(n // PROJ_CHUNK):
        sl = slice(c * PROJ_CHUNK, (c + 1) * PROJ_CHUNK)
        o_ref[:, sl] = (_dot(x, w_ref[:, sl]) * cs_ref[:, sl]).astype(o_ref.dtype)


def _proj_gate_kernel(x_ref, w_ref, cs_ref, wf_ref, o_ref, f_ref):
    _proj_kernel(x_ref, w_ref, cs_ref, o_ref)
    f_ref[...] = _dot(x_ref[...], wf_ref[...])


def _proj(xn, w, colscale, w_gate=None):
    m, k = xn.shape
    n = w.shape[1]
    in_specs = [pl.BlockSpec((ROW_TILE, k), lambda i: (i, 0)),
                pl.BlockSpec((k, n), lambda i: (0, 0)),
                pl.BlockSpec((1, n), lambda i: (0, 0))]
    out_specs = [pl.BlockSpec((ROW_TILE, n), lambda i: (i, 0))]
    out_shape = [jax.ShapeDtypeStruct((m, n), BF16)]
    args = [xn, w, colscale]
    body = _proj_kernel
    if w_gate is not None:
        in_specs.append(pl.BlockSpec((k, LANES), lambda i: (0, 0)))
        out_specs.append(pl.BlockSpec((ROW_TILE, LANES), lambda i: (i, 0)))
        out_shape.append(jax.ShapeDtypeStruct((m, LANES), F32))
        args.append(w_gate)
        body = _proj_gate_kernel
    return pl.pallas_call(
        body,
        grid=(m // ROW_TILE,),
        in_specs=in_specs,
        out_specs=out_specs,
        out_shape=out_shape,
        compiler_params=_params(("parallel",), 48),
        name="qkv_proj",
    )(*args)


def _softplus2(y):
    neg_abs = pltpu.bitcast(pltpu.bitcast(y, jnp.uint32) | jnp.uint32(0x80000000), F32)
    return jnp.maximum(y, 0.0) + jnp.log(1.0 + jnp.exp2(neg_abs)) * LOG2E


def _gate_cumsum_kernel(f_ref, b_ref, tri_ref, place_ref, o_ref):
    n_blocks = f_ref.shape[1] // CUM_BLOCK
    tri = tri_ref[...]
    lane = lax.broadcasted_iota(jnp.int32, (CUM_BLOCK, LANES), 1)
    ones = jnp.where((lane >= ONES_LANE) & (lane < ONES_LANE + GATE_PARTS), 1.0, 0.0)

    def body(r, carry):
        rows = pl.ds(pl.multiple_of(r * CUM_BLOCK, CUM_BLOCK), CUM_BLOCK)
        logit2 = (f_ref[0, rows, :] + b_ref[...]) * LOG2E
        ls = -_softplus2(-logit2)
        c2 = carry
        for piece in _split_bf16(ls, GATE_PARTS):
            c2 = c2 + _dot(tri, piece)
        operand = ones
        for i, piece in enumerate(_split_bf16(c2, GATE_PARTS)):
            operand = operand + _dot(piece, place_ref[i])
        o_ref[0, rows, :] = operand.astype(o_ref.dtype)
        return c2[CUM_BLOCK - 1:CUM_BLOCK, :]

    lax.fori_loop(0, n_blocks, body, jnp.zeros((1, LANES), F32))


def _gate_operand(f_logit, b_gate):
    b, s, _ = f_logit.shape
    idx = jnp.arange(CUM_BLOCK)
    tri = (idx[None, :] <= idx[:, None]).astype(BF16)
    lane = jnp.arange(LANES)
    place = jnp.stack([((lane[:, None] < N_HEADS) & (lane[None, :] == i * N_HEADS + lane[:, None]))
                       for i in range(GATE_PARTS)]).astype(BF16)
    return pl.pallas_call(
        _gate_cumsum_kernel,
        grid=(b,),
        in_specs=[pl.BlockSpec((1, s, LANES), lambda i: (i, 0, 0)),
                  pl.BlockSpec((1, LANES), lambda i: (0, 0)),
                  pl.BlockSpec((CUM_BLOCK, CUM_BLOCK), lambda i: (0, 0)),
                  pl.BlockSpec((GATE_PARTS, LANES, LANES), lambda i: (0, 0, 0))],
        out_specs=pl.BlockSpec((1, s, LANES), lambda i: (i, 0, 0)),
        out_shape=jax.ShapeDtypeStruct((b, s, LANES), BF16),
        compiler_params=_params(("parallel",), 32),
        name="gate_cumsum",
    )(f_logit, b_gate, tri, place)


def _stack_heads(q):
    lane = lax.broadcasted_iota(jnp.int32, q.shape, 1)
    zero = jnp.zeros_like(q)
    return jnp.concatenate([jnp.where(lane < HEAD_DIM, q, zero),
                            jnp.where(lane >= HEAD_DIM, q, zero)], axis=0)


def _causal(strict):
    shape = (TK, HEADS_PER_PAIR * TQ)
    key = lax.broadcasted_iota(jnp.int32, shape, 0)
    query = lax.broadcasted_iota(jnp.int32, shape, 1) & (TQ - 1)
    return key < query if strict else key <= query


def _rows(index, size):
    if isinstance(index, int):
        return pl.ds(index * size, size)
    return pl.ds(pl.multiple_of(index * size, size), size)


def _transpose_values(v_ref, store):
    row = lax.broadcasted_iota(jnp.int32, (LANES, LANES), 0)
    col = lax.broadcasted_iota(jnp.int32, (LANES, LANES), 1)
    eye = (row == col).astype(BF16)

    group = VT_GROUP // VT_BLOCK

    def body(i, carry):
        vt = _dot_nt(eye, v_ref[0, _rows(i, VT_GROUP), :]).astype(BF16)
        for g in range(group):
            store(i * group + g, vt[:, g * VT_BLOCK:(g + 1) * VT_BLOCK])
        return carry

    lax.fori_loop(0, v_ref.shape[1] // VT_GROUP, body, 0)


def _value_blocks(vt_ref, j, *lead):
    per_block = TK // VT_BLOCK
    return jnp.concatenate([vt_ref[(*lead, j * per_block + i)] for i in range(per_block)], axis=1)


def _pipelined_sweep(n_tiles, stages, descending):
    depth = len(stages)
    pairs = [(t, j) for t in range(n_tiles)
             for j in (range(t, -1, -1) if descending else range(t + 1))]
    n = len(pairs)
    assert n >= depth - 1 + PIPELINE_UNROLL and PIPELINE_UNROLL % 2 == 0

    def run(step_pairs, parity, masked, active):
        for s in active:
            slot = parity if s % 2 == 0 else 1 - parity
            tile, block = step_pairs[s]
            if s == 0:
                stages[0](tile, block, slot, masked)
            else:
                stages[s](tile, block, slot)

    def static_step(u):
        active = [s for s in range(depth) if 0 <= u - s < n]
        step_pairs = {s: pairs[u - s] for s in active}
        masked = 0 in step_pairs and step_pairs[0][0] == step_pairs[0][1]
        run(step_pairs, u & 1, masked, active)

    def advance(tile, block):
        if descending:
            last = block == 0
            return jnp.where(last, tile + 1, tile), jnp.where(last, tile + 1, block - 1)
        last = block == tile
        return jnp.where(last, tile + 1, tile), jnp.where(last, 0, block + 1)

    start = depth - 1 + (n - depth + 1) % PIPELINE_UNROLL
    for u in range(start):
        static_step(u)

    def on_diagonal(pair):
        return pair[0] == pair[1]

    mask_cases = sorted({tuple(on_diagonal(pairs[u + i]) for i in range(PIPELINE_UNROLL))
                         for u in range(start, n, PIPELINE_UNROLL)})

    def body(_, carry):
        window = [(carry[2 * s], carry[2 * s + 1]) for s in range(depth)]
        for _ in range(PIPELINE_UNROLL - 1):
            window.insert(0, advance(*window[0]))
        heads = window[:PIPELINE_UNROLL][::-1]
        for masked in mask_cases:
            cond = on_diagonal(heads[0]) == masked[0]
            for i in range(1, PIPELINE_UNROLL):
                cond = jnp.logical_and(cond, on_diagonal(heads[i]) == masked[i])

            @pl.when(cond)
            def _():
                for i in range(PIPELINE_UNROLL):
                    offset = PIPELINE_UNROLL - 1 - i
                    step_pairs = {s: window[offset + s] for s in range(depth)}
                    run(step_pairs, (start + i) & 1, masked[i], range(depth))
        window.insert(0, advance(*window[0]))
        return tuple(v for pair in window[:depth] for v in pair)

    init = ()
    for s in range(depth):
        init += tuple(jnp.int32(v) for v in pairs[start - s])
    lax.fori_loop(0, (n - start) // PIPELINE_UNROLL, body, init)

    for u in range(n, n + depth - 1):
        static_step(u)


def _store_outputs(o_ref, n_tiles, normalized):
    def body(t, carry):
        acc_t = normalized(t)
        out_t = jnp.concatenate([acc_t[:HEAD_DIM, :TQ], acc_t[HEAD_DIM:, TQ:]], axis=0)
        o_ref[0, _rows(t, TQ), :] = out_t.T.astype(o_ref.dtype)
        return carry

    lax.fori_loop(0, n_tiles, body, 0)


def _fox_kernel(q_ref, k_ref, v_ref, gate_ref, perm_ref, o_ref,
                vt_ref, qa_ref, m_ref, acc_ref, s_buf, m_buf, a_buf):
    pair = pl.program_id(1)
    n_tiles = q_ref.shape[1] // TQ

    row = lax.broadcasted_iota(jnp.int32, (LANES, VT_BLOCK), 0)

    def store(j, vt):
        one = jnp.ones_like(vt)
        vt_ref[0, j] = jnp.where(row < HEAD_DIM, vt, one)
        vt_ref[1, j] = jnp.where(row >= HEAD_DIM, vt, one)

    _transpose_values(v_ref, store)

    q = q_ref[0]
    gate_all = gate_ref[0]
    zero = jnp.zeros_like(q)
    lane = lax.broadcasted_iota(jnp.int32, q.shape, 1)
    lane_row = lax.broadcasted_iota(jnp.int32, (1, LANES), 1)
    for hh in range(HEADS_PER_PAIR):
        h = HEADS_PER_PAIR * pair + hh
        minus = jnp.zeros((1, LANES), F32)
        for i in range(GATE_PARTS):
            minus = jnp.where(lane_row == i * N_HEADS + h, -1.0, minus)
        gate_lanes = (_dot(gate_all, perm_ref[h]) + minus).astype(BF16)
        feat = (lane < HEAD_DIM) if hh == 0 else (lane >= HEAD_DIM)
        operand = jnp.concatenate([jnp.where(feat, q, zero), gate_lanes], axis=1)
        qa_ref[:, hh * TQ:(hh + 1) * TQ, :] = operand.reshape(n_tiles, TQ, 2 * LANES)
    m_ref[...] = jnp.full(m_ref.shape, -jnp.inf, F32)
    acc_ref[...] = jnp.zeros(acc_ref.shape, F32)

    def scores(t, j, slot, masked):
        rows = _rows(j, TK)
        k_aug = jnp.concatenate([k_ref[0, rows, :], gate_ref[0, rows, :]], axis=1)
        s = _dot_nt(k_aug, qa_ref[t])
        if masked:
            s = jnp.where(_causal(strict=False), s, -jnp.inf)
        m_old = m_ref[t]
        m_new = jnp.maximum(m_old, jnp.max(s, axis=0, keepdims=True))
        m_ref[t] = m_new
        m_buf[slot] = m_new
        a_buf[slot] = jnp.exp2(m_old - m_new)
        s_buf[slot] = s

    def accumulate(t, j, slot):
        prob = jnp.exp2(s_buf[slot] - m_buf[slot]).astype(BF16)
        pv = jnp.concatenate(
            [_dot(_value_blocks(vt_ref, j, hh), prob[:, hh * TQ:(hh + 1) * TQ])
             for hh in range(HEADS_PER_PAIR)], axis=1)
        acc_ref[t] = acc_ref[t] * a_buf[slot] + pv

    _pipelined_sweep(n_tiles, (scores, accumulate), descending=False)

    def normalized(t):
        acc = acc_ref[t]
        denom = jnp.concatenate(
            [jnp.broadcast_to(acc[HEAD_DIM:HEAD_DIM + 1, :TQ], (LANES, TQ)),
             jnp.broadcast_to(acc[0:1, TQ:], (LANES, TQ))], axis=1)
        return acc / denom

    _store_outputs(o_ref, n_tiles, normalized)


def _sb_kernel(q_ref, k_ref, v_ref, tt_ref, o_ref,
               vt_ref, qs_ref, r_ref, acc_ref, z_buf, w_buf):
    n_tiles = q_ref.shape[1] // TQ

    def store(j, vt):
        vt_ref[j] = vt

    _transpose_values(v_ref, store)

    q = q_ref[0]
    zero = jnp.zeros_like(q)
    lane = lax.broadcasted_iota(jnp.int32, q.shape, 1)
    for hh in range(HEADS_PER_PAIR):
        feat = (lane < HEAD_DIM) if hh == 0 else (lane >= HEAD_DIM)
        qs_ref[:, hh * TQ:(hh + 1) * TQ, :] = jnp.where(feat, q, zero).reshape(n_tiles, TQ, LANES)
    r_ref[...] = jnp.zeros(r_ref.shape, F32)
    acc_ref[...] = jnp.zeros(acc_ref.shape, F32)

    n_sub = TK // SB_BLOCK

    def logits(t, j, slot, masked):
        z = _dot_nt(k_ref[0, _rows(j, TK), :], qs_ref[t])
        if masked:
            z = jnp.where(_causal(strict=True), z, -jnp.inf)
        z_buf[slot] = z

    def weights(t, j, slot):
        r = r_ref[t]
        for g in reversed(range(n_sub)):
            rows = slice(g * SB_BLOCK, (g + 1) * SB_BLOCK)
            z = z_buf[slot, rows, :]
            sp = _softplus2(z)
            sums = _dot(tt_ref[...], sp.astype(BF16))
            w_buf[slot, rows, :] = jnp.exp2(z - sp - sums[:SB_BLOCK] - r).astype(BF16)
            r = r + sums[SB_BLOCK:SB_BLOCK + 1]
        r_ref[t] = r

    def accumulate(t, j, slot):
        acc_ref[t] += _dot(_value_blocks(vt_ref, j), w_buf[slot])

    _pipelined_sweep(n_tiles, (logits, weights, accumulate), descending=True)
    _store_outputs(o_ref, n_tiles, lambda t: acc_ref[t])


def _attention_call(body, name, qkv, extra_inputs, extra_specs, scratch_shapes):
    b, s, _ = qkv.shape

    def column_block(offset):
        return pl.BlockSpec((1, s, LANES), lambda b, p: (b, 0, offset + p))

    return pl.pallas_call(
        body,
        grid=(b, N_HEAD_PAIRS),
        in_specs=[column_block(0), column_block(N_HEAD_PAIRS), column_block(2 * N_HEAD_PAIRS)
                  ] + extra_specs,
        out_specs=column_block(0),
        out_shape=jax.ShapeDtypeStruct((b, s, D_MODEL), BF16),
        scratch_shapes=scratch_shapes,
        compiler_params=_params(("parallel", "parallel"), 56),
        name=name,
    )(qkv, qkv, qkv, *extra_inputs)


def _fox_attention(qkv, gate):
    b, s, _ = qkv.shape
    n_tiles = s // TQ
    lane = jnp.arange(LANES)
    perm = jnp.stack([
        sum(((lane[:, None] == i * N_HEADS + h) & (lane[None, :] == ONES_LANE + i))
            for i in range(GATE_PARTS))
        for h in range(N_HEADS)]).astype(BF16)
    cols = HEADS_PER_PAIR * TQ
    return _attention_call(
        _fox_kernel, "fox_attention", qkv, [gate, perm],
        [pl.BlockSpec((1, s, LANES), lambda b, p: (b, 0, 0)),
         pl.BlockSpec((N_HEADS, LANES, LANES), lambda b, p: (0, 0, 0))],
        [pltpu.VMEM((HEADS_PER_PAIR, s // VT_BLOCK, LANES, VT_BLOCK), BF16),
         pltpu.VMEM((n_tiles, cols, 2 * LANES), BF16),
         pltpu.VMEM((n_tiles, 1, cols), F32),
         pltpu.VMEM((n_tiles, LANES, cols), F32),
         pltpu.VMEM((2, TK, cols), F32),
         pltpu.VMEM((2, 1, cols), F32),
         pltpu.VMEM((2, 1, cols), F32)])


def _sb_attention(qkv):
    b, s, _ = qkv.shape
    n_tiles = s // TQ
    idx = jnp.arange(SB_BLOCK)
    later = (idx[None, :] > idx[:, None]).astype(BF16)
    tt = jnp.concatenate([later, jnp.ones((SUBLANES, SB_BLOCK), BF16)], axis=0)
    cols = HEADS_PER_PAIR * TQ
    return _attention_call(
        _sb_kernel, "sb_attention", qkv, [tt],
        [pl.BlockSpec((SB_BLOCK + SUBLANES, SB_BLOCK), lambda b, p: (0, 0))],
        [pltpu.VMEM((s // VT_BLOCK, LANES, VT_BLOCK), BF16),
         pltpu.VMEM((n_tiles, cols, LANES), BF16),
         pltpu.VMEM((n_tiles, 1, cols), F32),
         pltpu.VMEM((n_tiles, LANES, cols), F32),
         pltpu.VMEM((2, TK, cols), F32),
         pltpu.VMEM((2, TK, cols), BF16)])


def _mm_resid_norm_kernel(a_ref, w_ref, h_ref, g_ref, hs_ref, xn_ref):
    h = _dot(a_ref[...], w_ref[...]) + h_ref[...]
    hs_ref[...] = h
    xn_ref[...] = _rms(h, g_ref[...]).astype(xn_ref.dtype)


def _mm_resid_final_kernel(a_ref, w_ref, h_ref, g_ref, o_ref):
    h = _dot(a_ref[...], w_ref[...]) + h_ref[...]
    o_ref[...] = _rms(h, g_ref[...]).astype(o_ref.dtype)


def _mm_resid_norm(a, w, hs, g, final=False):
    m, k = a.shape
    d = w.shape[1]
    row = lambda i: (i, 0)
    in_specs = [pl.BlockSpec((ROW_TILE, k), row),
                pl.BlockSpec((k, d), lambda i: (0, 0)),
                pl.BlockSpec((ROW_TILE, d), row),
                pl.BlockSpec((1, d), lambda i: (0, 0))]
    if final:
        body = _mm_resid_final_kernel
        out_specs = pl.BlockSpec((ROW_TILE, d), row)
        out_shape = jax.ShapeDtypeStruct((m, d), F32)
    else:
        body = _mm_resid_norm_kernel
        out_specs = [pl.BlockSpec((ROW_TILE, d), row), pl.BlockSpec((ROW_TILE, d), row)]
        out_shape = [jax.ShapeDtypeStruct((m, d), F32), jax.ShapeDtypeStruct((m, d), BF16)]
    return pl.pallas_call(
        body,
        grid=(m // ROW_TILE,),
        in_specs=in_specs,
        out_specs=out_specs,
        out_shape=out_shape,
        compiler_params=_params(("parallel",), 48),
        name="matmul_residual_norm",
    )(a, w, hs, g.reshape(1, d))


def _ffn_up_kernel(x_ref, w_ref, wc_ref, bc_ref, o_ref, carry_ref, *, tiles_per_seq):
    i = pl.program_id(0)

    @pl.when(i % tiles_per_seq == 0)
    def _():
        carry_ref[...] = jnp.zeros(carry_ref.shape, F32)

    row = lax.broadcasted_iota(jnp.int32, (SUBLANES, 2 * FF_CHUNK), 0)

    def shifted(h, prev, shift):
        rolled = pltpu.roll(h, shift, 0)
        head = rolled[:SUBLANES]
        for i in range(shift):
            head = jnp.where(row == i, prev[SUBLANES - shift + i:SUBLANES - shift + i + 1], head)
        return jnp.concatenate([head, rolled[SUBLANES:]], axis=0)

    for r in range(x_ref.shape[0] // FF_ROWS):
        rows = slice(r * FF_ROWS, (r + 1) * FF_ROWS)
        x = x_ref[rows, :]
        for c in range(N_FF_CHUNKS):
            h = _dot(x, w_ref[c])
            prev = carry_ref[c]
            h1 = shifted(h, prev, 1)
            h2 = shifted(h, prev, 2)
            wc = wc_ref[c]
            hc = bc_ref[c] + wc[0:1] * h2 + wc[1:2] * h1 + wc[2:3] * h
            u = hc[:, :FF_CHUNK]
            g = hc[:, FF_CHUNK:]
            act = g * u / (1.0 + jnp.exp(-g))
            o_ref[rows, c * FF_CHUNK:(c + 1) * FF_CHUNK] = act.astype(o_ref.dtype)
            carry_ref[c] = h[FF_ROWS - SUBLANES:FF_ROWS]


def _ffn_up(xn, w_chunks, wc_chunks, bc_chunks, seq_len):
    m, k = xn.shape
    kern = functools.partial(_ffn_up_kernel, tiles_per_seq=seq_len // ROW_TILE)
    return pl.pallas_call(
        kern,
        grid=(m // ROW_TILE,),
        in_specs=[pl.BlockSpec((ROW_TILE, k), lambda i: (i, 0)),
                  pl.BlockSpec((N_FF_CHUNKS, k, 2 * FF_CHUNK), lambda i: (0, 0, 0)),
                  pl.BlockSpec((N_FF_CHUNKS, CONV_WIDTH, 2 * FF_CHUNK), lambda i: (0, 0, 0)),
                  pl.BlockSpec((N_FF_CHUNKS, 1, 2 * FF_CHUNK), lambda i: (0, 0, 0))],
        out_specs=pl.BlockSpec((ROW_TILE, D_FF), lambda i: (i, 0)),
        out_shape=jax.ShapeDtypeStruct((m, D_FF), BF16),
        scratch_shapes=[pltpu.VMEM((N_FF_CHUNKS, SUBLANES, 2 * FF_CHUNK), F32)],
        compiler_params=_params(("arbitrary",), 56),
        name="ffn_up_conv_gate",
    )(xn, w_chunks, wc_chunks, bc_chunks)


def _chunk_columns(t):
    r = t.shape[0]
    t = t.reshape(r, 2, N_FF_CHUNKS, FF_CHUNK)
    return t.transpose(2, 0, 1, 3).reshape(N_FF_CHUNKS, r, 2 * FF_CHUNK)


def kernel(x, attn_norm, ffn_norm, final_norm, fox_w_qkvf, fox_b_f, fox_w_o,
           sb_w_qkv, sb_w_o, ffn_w_up, ffn_w_conv, ffn_b_conv, ffn_w_down):
    b, s, d = x.shape
    m = b * s
    assert d == D_MODEL and s % TQ == 0 and s % ROW_TILE == 0 and s % VT_GROUP == 0 and TQ == TK

    colscale = jnp.concatenate([jnp.full((1, d), ATTN_SCALE * LOG2E, F32),
                                jnp.ones((1, 2 * d), F32)], axis=1)

    hs = x.reshape(m, d)
    xn = _rmsnorm(hs, attn_norm[0], BF16)
    out = None
    for i in range(DEPTH):
        j = i // 2
        if i % 2 == 0:
            w = fox_w_qkvf[j]
            w_gate = jnp.pad(w[:, 3 * d:], ((0, 0), (0, LANES - N_HEADS))).astype(BF16)
            qkv, f_logit = _proj(xn, w[:, :3 * d].astype(BF16), colscale, w_gate)
            b_gate = jnp.pad(fox_b_f[j], (0, LANES - N_HEADS)).reshape(1, LANES)
            gate = _gate_operand(f_logit.reshape(b, s, LANES), b_gate)
            o = _fox_attention(qkv.reshape(b, s, 3 * d), gate)
            w_o = fox_w_o[j]
        else:
            qkv = _proj(xn, sb_w_qkv[j].astype(BF16), colscale)[0]
            o = _sb_attention(qkv.reshape(b, s, 3 * d))
            w_o = sb_w_o[j]
        hs, xn = _mm_resid_norm(o.reshape(m, d), w_o.astype(BF16), hs, ffn_norm[i])
        act = _ffn_up(xn,
                      _chunk_columns(ffn_w_up[i]).astype(BF16),
                      _chunk_columns(ffn_w_conv[i]),
                      _chunk_columns(ffn_b_conv[i].reshape(1, -1)),
                      s)
        w_down = ffn_w_down[i].astype(BF16)
        if i + 1 < DEPTH:
            hs, xn = _mm_resid_norm(act, w_down, hs, attn_norm[i + 1])
        else:
            out = _mm_resid_norm(act, w_down, hs, final_norm, final=True)
    return out.reshape(b, s, d)
```

```python
import functools
import math

import jax
import jax.numpy as jnp
from jax import lax
from jax.experimental import pallas as pl
from jax.experimental.pallas import tpu as pltpu

D_MODEL = 1024
N_HEADS = 16
HEAD_DIM = D_MODEL // N_HEADS
D_FF = 2816
CONV_WIDTH = 3
DEPTH = 4
NORM_EPS = 1e-6
ATTN_SCALE = HEAD_DIM ** -0.5
LOG2E = math.log2(math.e)

F32 = jnp.float32
BF16 = jnp.bfloat16

LANES = 128
SUBLANES = 8
HEADS_PER_PAIR = LANES // HEAD_DIM
N_HEAD_PAIRS = N_HEADS // HEADS_PER_PAIR
ROW_TILE = 512
PROJ_CHUNK = 512
FF_CHUNK = 256
N_FF_CHUNKS = D_FF // FF_CHUNK
FF_ROWS = 128
TQ = 512
TK = 512
VT_BLOCK = 256
VT_GROUP = 1024
PIPELINE_UNROLL = 4
CUM_BLOCK = 128
SB_BLOCK = 256
GATE_PARTS = 3
ONES_LANE = GATE_PARTS * N_HEADS
MIB = 2 ** 20


def _params(semantics, vmem_mib):
    return pltpu.CompilerParams(dimension_semantics=semantics,
                                vmem_limit_bytes=vmem_mib * MIB)


def _dot(a, b):
    return jnp.dot(a, b, preferred_element_type=F32)


def _dot_nt(a, b):
    return lax.dot_general(a, b, (((1,), (1,)), ((), ())), preferred_element_type=F32)


def _split_bf16(x, parts):
    pieces = []
    for _ in range(parts - 1):
        p = x.astype(BF16)
        pieces.append(p)
        x = x - p.astype(F32)
    pieces.append(x.astype(BF16))
    return pieces


def _rms(h, g):
    ms = jnp.mean(h * h, axis=-1, keepdims=True)
    return h * lax.rsqrt(ms + NORM_EPS) * g


def _rmsnorm_kernel(x_ref, g_ref, o_ref):
    o_ref[...] = _rms(x_ref[...], g_ref[...]).astype(o_ref.dtype)


def _rmsnorm(x, g, out_dtype):
    m, d = x.shape
    return pl.pallas_call(
        _rmsnorm_kernel,
        grid=(m // ROW_TILE,),
        in_specs=[pl.BlockSpec((ROW_TILE, d), lambda i: (i, 0)),
                  pl.BlockSpec((1, d), lambda i: (0, 0))],
        out_specs=pl.BlockSpec((ROW_TILE, d), lambda i: (i, 0)),
        out_shape=jax.ShapeDtypeStruct((m, d), out_dtype),
        compiler_params=_params(("parallel",), 32),
        name="rmsnorm",
    )(x, g.reshape(1, d))


def _proj_kernel(x_ref, w_ref, cs_ref, o_ref):
    x = x_ref[...]
    n = w_ref.shape[1]
    for c in range(n // PROJ_CHUNK):
        sl = slice(c * PROJ_CHUNK, (c + 1) * PROJ_CHUNK)
        o_ref[:, sl] = (_dot(x, w_ref[:, sl]) * cs_ref[:, sl]).astype(o_ref.dtype)


def _proj_gate_kernel(x_ref, w_ref, cs_ref, wf_ref, o_ref, f_ref):
    _proj_kernel(x_ref, w_ref, cs_ref, o_ref)
    f_ref[...] = _dot(x_ref[...], wf_ref[...])


def _proj(xn, w, colscale, w_gate=None):
    m, k = xn.shape
    n = w.shape[1]
    in_specs = [pl.BlockSpec((ROW_TILE, k), lambda i: (i, 0)),
                pl.BlockSpec((k, n), lambda i: (0, 0)),
                pl.BlockSpec((1, n), lambda i: (0, 0))]
    out_specs = [pl.BlockSpec((ROW_TILE, n), lambda i: (i, 0))]
    out_shape = [jax.ShapeDtypeStruct((m, n), BF16)]
    args = [xn, w, colscale]
    body = _proj_kernel
    if w_gate is not None:
        in_specs.append(pl.BlockSpec((k, LANES), lambda i: (0, 0)))
        out_specs.append(pl.BlockSpec((ROW_TILE, LANES), lambda i: (i, 0)))
        out_shape.append(jax.ShapeDtypeStruct((m, LANES), F32))
        args.append(w_gate)
        body = _proj_gate_kernel
    return pl.pallas_call(
        body,
        grid=(m // ROW_TILE,),
        in_specs=in_specs,
        out_specs=out_specs,
        out_shape=out_shape,
        compiler_params=_params(("parallel",), 48),
        name="qkv_proj",
    )(*args)


def _softplus2(y):
    neg_abs = pltpu.bitcast(pltpu.bitcast(y, jnp.uint32) | jnp.uint32(0x80000000), F32)
    return jnp.maximum(y, 0.0) + jnp.log(1.0 + jnp.exp2(neg_abs)) * LOG2E


def _gate_cumsum_kernel(f_ref, b_ref, tri_ref, place_ref, o_ref):
    n_blocks = f_ref.shape[1] // CUM_BLOCK
    tri = tri_ref[...]
    lane = lax.broadcasted_iota(jnp.int32, (CUM_BLOCK, LANES), 1)
    ones = jnp.where((lane >= ONES_LANE) & (lane < ONES_LANE + GATE_PARTS), 1.0, 0.0)

    def body(r, carry):
        rows = pl.ds(pl.multiple_of(r * CUM_BLOCK, CUM_BLOCK), CUM_BLOCK)
        logit2 = (f_ref[0, rows, :] + b_ref[...]) * LOG2E
        ls = -_softplus2(-logit2)
        c2 = carry
        for piece in _split_bf16(ls, GATE_PARTS):
            c2 = c2 + _dot(tri, piece)
        operand = ones
        for i, piece in enumerate(_split_bf16(c2, GATE_PARTS)):
            operand = operand + _dot(piece, place_ref[i])
        o_ref[0, rows, :] = operand.astype(o_ref.dtype)
        return c2[CUM_BLOCK - 1:CUM_BLOCK, :]

    lax.fori_loop(0, n_blocks, body, jnp.zeros((1, LANES), F32))


def _gate_operand(f_logit, b_gate):
    b, s, _ = f_logit.shape
    idx = jnp.arange(CUM_BLOCK)
    tri = (idx[None, :] <= idx[:, None]).astype(BF16)
    lane = jnp.arange(LANES)
    place = jnp.stack([((lane[:, None] < N_HEADS) & (lane[None, :] == i * N_HEADS + lane[:, None]))
                       for i in range(GATE_PARTS)]).astype(BF16)
    return pl.pallas_call(
        _gate_cumsum_kernel,
        grid=(b,),
        in_specs=[pl.BlockSpec((1, s, LANES), lambda i: (i, 0, 0)),
                  pl.BlockSpec((1, LANES), lambda i: (0, 0)),
                  pl.BlockSpec((CUM_BLOCK, CUM_BLOCK), lambda i: (0, 0)),
                  pl.BlockSpec((GATE_PARTS, LANES, LANES), lambda i: (0, 0, 0))],
        out_specs=pl.BlockSpec((1, s, LANES), lambda i: (i, 0, 0)),
        out_shape=jax.ShapeDtypeStruct((b, s, LANES), BF16),
        compiler_params=_params(("parallel",), 32),
        name="gate_cumsum",
    )(f_logit, b_gate, tri, place)


def _causal(strict):
    shape = (TK, HEADS_PER_PAIR * TQ)
    key = lax.broadcasted_iota(jnp.int32, shape, 0)
    query = lax.broadcasted_iota(jnp.int32, shape, 1) & (TQ - 1)
    return key < query if strict else key <= query


def _rows(index, size):
    if isinstance(index, int):
        return pl.ds(index * size, size)
    return pl.ds(pl.multiple_of(index * size, size), size)


def _transpose_values(v_ref, store):
    row = lax.broadcasted_iota(jnp.int32, (LANES, LANES), 0)
    col = lax.broadcasted_iota(jnp.int32, (LANES, LANES), 1)
    eye = (row == col).astype(BF16)

    group = VT_GROUP // VT_BLOCK

    def body(i, carry):
        vt = _dot_nt(eye, v_ref[0, _rows(i, VT_GROUP), :]).astype(BF16)
        for g in range(group):
            store(i * group + g, vt[:, g * VT_BLOCK:(g + 1) * VT_BLOCK])
        return carry

    lax.fori_loop(0, v_ref.shape[1] // VT_GROUP, body, 0)


def _value_blocks(vt_ref, j, *lead):
    per_block = TK // VT_BLOCK
    return jnp.concatenate([vt_ref[(*lead, j * per_block + i)] for i in range(per_block)], axis=1)


def _pipelined_sweep(n_tiles, stages, descending):
    depth = len(stages)
    pairs = [(t, j) for t in range(n_tiles)
             for j in (range(t, -1, -1) if descending else range(t + 1))]
    n = len(pairs)
    assert n >= depth - 1 + PIPELINE_UNROLL and PIPELINE_UNROLL % 2 == 0

    def run(step_pairs, parity, masked, active):
        for s in active:
            slot = parity if s % 2 == 0 else 1 - parity
            tile, block = step_pairs[s]
            if s == 0:
                stages[0](tile, block, slot, masked)
            else:
                stages[s](tile, block, slot)

    def static_step(u):
        active = [s for s in range(depth) if 0 <= u - s < n]
        step_pairs = {s: pairs[u - s] for s in active}
        masked = 0 in step_pairs and step_pairs[0][0] == step_pairs[0][1]
        run(step_pairs, u & 1, masked, active)

    def advance(tile, block):
        if descending:
            last = block == 0
            return jnp.where(last, tile + 1, tile), jnp.where(last, tile + 1, block - 1)
        last = block == tile
        return jnp.where(last, tile + 1, tile), jnp.where(last, 0, block + 1)

    start = depth - 1 + (n - depth + 1) % PIPELINE_UNROLL
    for u in range(start):
        static_step(u)

    def on_diagonal(pair):
        return pair[0] == pair[1]

    mask_cases = sorted({tuple(on_diagonal(pairs[u + i]) for i in range(PIPELINE_UNROLL))
                         for u in range(start, n, PIPELINE_UNROLL)})

    def body(_, carry):
        window = [(carry[2 * s], carry[2 * s + 1]) for s in range(depth)]
        for _ in range(PIPELINE_UNROLL - 1):
            window.insert(0, advance(*window[0]))
        heads = window[:PIPELINE_UNROLL][::-1]
        for masked in mask_cases:
            cond = on_diagonal(heads[0]) == masked[0]
            for i in range(1, PIPELINE_UNROLL):
                cond = jnp.logical_and(cond, on_diagonal(heads[i]) == masked[i])

            @pl.when(cond)
            def _():
                for i in range(PIPELINE_UNROLL):
                    offset = PIPELINE_UNROLL - 1 - i
                    step_pairs = {s: window[offset + s] for s in range(depth)}
                    run(step_pairs, (start + i) & 1, masked[i], range(depth))
        window.insert(0, advance(*window[0]))
        return tuple(v for pair in window[:depth] for v in pair)

    init = ()
    for s in range(depth):
        init += tuple(jnp.int32(v) for v in pairs[start - s])
    lax.fori_loop(0, (n - start) // PIPELINE_UNROLL, body, init)

    for u in range(n, n + depth - 1):
        static_step(u)


def _store_outputs(o_ref, n_tiles, normalized):
    def body(t, carry):
        acc_t = normalized(t)
        out_t = jnp.concatenate([acc_t[:HEAD_DIM, :TQ], acc_t[HEAD_DIM:, TQ:]], axis=0)
        o_ref[0, _rows(t, TQ), :] = out_t.T.astype(o_ref.dtype)
        return carry

    lax.fori_loop(0, n_tiles, body, 0)


def _fox_kernel(q_ref, k_ref, v_ref, gate_ref, perm_ref, o_ref,
                vt_ref, qa_ref, m_ref, acc_ref, s_buf, m_buf, a_buf):
    pair = pl.program_id(1)
    n_tiles = q_ref.shape[1] // TQ

    row = lax.broadcasted_iota(jnp.int32, (LANES, VT_BLOCK), 0)

    def store(j, vt):
        one = jnp.ones_like(vt)
        vt_ref[0, j] = jnp.where(row < HEAD_DIM, vt, one)
        vt_ref[1, j] = jnp.where(row >= HEAD_DIM, vt, one)

    _transpose_values(v_ref, store)

    q = q_ref[0]
    gate_all = gate_ref[0]
    zero = jnp.zeros_like(q)
    lane = lax.broadcasted_iota(jnp.int32, q.shape, 1)
    lane_row = lax.broadcasted_iota(jnp.int32, (1, LANES), 1)
    for hh in range(HEADS_PER_PAIR):
        h = HEADS_PER_PAIR * pair + hh
        minus = jnp.zeros((1, LANES), F32)
        for i in range(GATE_PARTS):
            minus = jnp.where(lane_row == i * N_HEADS + h, -1.0, minus)
        gate_lanes = (_dot(gate_all, perm_ref[h]) + minus).astype(BF16)
        feat = (lane < HEAD_DIM) if hh == 0 else (lane >= HEAD_DIM)
        operand = jnp.concatenate([jnp.where(feat, q, zero), gate_lanes], axis=1)
        qa_ref[:, hh * TQ:(hh + 1) * TQ, :] = operand.reshape(n_tiles, TQ, 2 * LANES)
    m_ref[...] = jnp.full(m_ref.shape, -jnp.inf, F32)
    acc_ref[...] = jnp.zeros(acc_ref.shape, F32)

    def scores(t, j, slot, masked):
        rows = _rows(j, TK)
        k_aug = jnp.concatenate([k_ref[0, rows, :], gate_ref[0, rows, :]], axis=1)
        s = _dot_nt(k_aug, qa_ref[t])
        if masked:
            s = jnp.where(_causal(strict=False), s, -jnp.inf)
        m_old = m_ref[t]
        m_new = jnp.maximum(m_old, jnp.max(s, axis=0, keepdims=True))
        m_ref[t] = m_new
        m_buf[slot] = m_new
        a_buf[slot] = jnp.exp2(m_old - m_new)
        s_buf[slot] = s

    def accumulate(t, j, slot):
        prob = jnp.exp2(s_buf[slot] - m_buf[slot]).astype(BF16)
        pv = jnp.concatenate(
            [_dot(_value_blocks(vt_ref, j, hh), prob[:, hh * TQ:(hh + 1) * TQ])
             for hh in range(HEADS_PER_PAIR)], axis=1)
        acc_ref[t] = acc_ref[t] * a_buf[slot] + pv

    _pipelined_sweep(n_tiles, (scores, accumulate), descending=False)

    def normalized(t):
        acc = acc_ref[t]
        denom = jnp.concatenate(
            [jnp.broadcast_to(acc[HEAD_DIM:HEAD_DIM + 1, :TQ], (LANES, TQ)),
             jnp.broadcast_to(acc[0:1, TQ:], (LANES, TQ))], axis=1)
        return acc / denom

    _store_outputs(o_ref, n_tiles, normalized)


def _sb_kernel(q_ref, k_ref, v_ref, tt_ref, o_ref,
               vt_ref, qs_ref, r_ref, acc_ref, z_buf, w_buf):
    n_tiles = q_ref.shape[1] // TQ

    def store(j, vt):
        vt_ref[j] = vt

    _transpose_values(v_ref, store)

    q = q_ref[0]
    zero = jnp.zeros_like(q)
    lane = lax.broadcasted_iota(jnp.int32, q.shape, 1)
    for hh in range(HEADS_PER_PAIR):
        feat = (lane < HEAD_DIM) if hh == 0 else (lane >= HEAD_DIM)
        qs_ref[:, hh * TQ:(hh + 1) * TQ, :] = jnp.where(feat, q, zero).reshape(n_tiles, TQ, LANES)
    r_ref[...] = jnp.zeros(r_ref.shape, F32)
    acc_ref[...] = jnp.zeros(acc_ref.shape, F32)

    n_sub = TK // SB_BLOCK

    def logits(t, j, slot, masked):
        z = _dot_nt(k_ref[0, _rows(j, TK), :], qs_ref[t])
        if masked:
            z = jnp.where(_causal(strict=True), z, -jnp.inf)
        z_buf[slot] = z

    def weights(t, j, slot):
        r = r_ref[t]
        for g in reversed(range(n_sub)):
            rows = slice(g * SB_BLOCK, (g + 1) * SB_BLOCK)
            z = z_buf[slot, rows, :]
            sp = _softplus2(z)
            sums = _dot(tt_ref[...], sp.astype(BF16))
            w_buf[slot, rows, :] = jnp.exp2(z - sp - sums[:SB_BLOCK] - r).astype(BF16)
            r = r + sums[SB_BLOCK:SB_BLOCK + 1]
        r_ref[t] = r

    def accumulate(t, j, slot):
        acc_ref[t] += _dot(_value_blocks(vt_ref, j), w_buf[slot])

    _pipelined_sweep(n_tiles, (logits, weights, accumulate), descending=True)
    _store_outputs(o_ref, n_tiles, lambda t: acc_ref[t])


def _attention_call(body, name, qkv, extra_inputs, extra_specs, scratch_shapes):
    b, s, _ = qkv.shape

    def column_block(offset):
        return pl.BlockSpec((1, s, LANES), lambda b, p: (b, 0, offset + p))

    return pl.pallas_call(
        body,
        grid=(b, N_HEAD_PAIRS),
        in_specs=[column_block(0), column_block(N_HEAD_PAIRS), column_block(2 * N_HEAD_PAIRS)
                  ] + extra_specs,
        out_specs=column_block(0),
        out_shape=jax.ShapeDtypeStruct((b, s, D_MODEL), BF16),
        scratch_shapes=scratch_shapes,
        compiler_params=_params(("parallel", "parallel"), 56),
        name=name,
    )(qkv, qkv, qkv, *extra_inputs)


def _fox_attention(qkv, gate):
    b, s, _ = qkv.shape
    n_tiles = s // TQ
    lane = jnp.arange(LANES)
    perm = jnp.stack([
        sum(((lane[:, None] == i * N_HEADS + h) & (lane[None, :] == ONES_LANE + i))
            for i in range(GATE_PARTS))
        for h in range(N_HEADS)]).astype(BF16)
    cols = HEADS_PER_PAIR * TQ
    return _attention_call(
        _fox_kernel, "fox_attention", qkv, [gate, perm],
        [pl.BlockSpec((1, s, LANES), lambda b, p: (b, 0, 0)),
         pl.BlockSpec((N_HEADS, LANES, LANES), lambda b, p: (0, 0, 0))],
        [pltpu.VMEM((HEADS_PER_PAIR, s // VT_BLOCK, LANES, VT_BLOCK), BF16),
         pltpu.VMEM((n_tiles, cols, 2 * LANES), BF16),
         pltpu.VMEM((n_tiles, 1, cols), F32),
         pltpu.VMEM((n_tiles, LANES, cols), F32),
         pltpu.VMEM((2, TK, cols), F32),
         pltpu.VMEM((2, 1, cols), F32),
         pltpu.VMEM((2, 1, cols), F32)])


def _sb_attention(qkv):
    b, s, _ = qkv.shape
    n_tiles = s // TQ
    idx = jnp.arange(SB_BLOCK)
    later = (idx[None, :] > idx[:, None]).astype(BF16)
    tt = jnp.concatenate([later, jnp.ones((SUBLANES, SB_BLOCK), BF16)], axis=0)
    cols = HEADS_PER_PAIR * TQ
    return _attention_call(
        _sb_kernel, "sb_attention", qkv, [tt],
        [pl.BlockSpec((SB_BLOCK + SUBLANES, SB_BLOCK), lambda b, p: (0, 0))],
        [pltpu.VMEM((s // VT_BLOCK, LANES, VT_BLOCK), BF16),
         pltpu.VMEM((n_tiles, cols, LANES), BF16),
         pltpu.VMEM((n_tiles, 1, cols), F32),
         pltpu.VMEM((n_tiles, LANES, cols), F32),
         pltpu.VMEM((2, TK, cols), F32),
         pltpu.VMEM((2, TK, cols), BF16)])


def _mm_resid_norm_kernel(a_ref, w_ref, h_ref, g_ref, hs_ref, xn_ref):
    h = _dot(a_ref[...], w_ref[...]) + h_ref[...]
    hs_ref[...] = h
    xn_ref[...] = _rms(h, g_ref[...]).astype(xn_ref.dtype)


def _mm_resid_final_kernel(a_ref, w_ref, h_ref, g_ref, o_ref):
    h = _dot(a_ref[...], w_ref[...]) + h_ref[...]
    o_ref[...] = _rms(h, g_ref[...]).astype(o_ref.dtype)


def _mm_resid_norm(a, w, hs, g, final=False):
    m, k = a.shape
    d = w.shape[1]
    row = lambda i: (i, 0)
    in_specs = [pl.BlockSpec((ROW_TILE, k), row),
                pl.BlockSpec((k, d), lambda i: (0, 0)),
                pl.BlockSpec((ROW_TILE, d), row),
                pl.BlockSpec((1, d), lambda i: (0, 0))]
    if final:
        body = _mm_resid_final_kernel
        out_specs = pl.BlockSpec((ROW_TILE, d), row)
        out_shape = jax.ShapeDtypeStruct((m, d), F32)
    else:
        body = _mm_resid_norm_kernel
        out_specs = [pl.BlockSpec((ROW_TILE, d), row), pl.BlockSpec((ROW_TILE, d), row)]
        out_shape = [jax.ShapeDtypeStruct((m, d), F32), jax.ShapeDtypeStruct((m, d), BF16)]
    return pl.pallas_call(
        body,
        grid=(m // ROW_TILE,),
        in_specs=in_specs,
        out_specs=out_specs,
        out_shape=out_shape,
        compiler_params=_params(("parallel",), 48),
        name="matmul_residual_norm",
    )(a, w, hs, g.reshape(1, d))


def _ffn_up_kernel(x_ref, w_ref, wc_ref, bc_ref, o_ref, carry_ref, *, tiles_per_seq):
    i = pl.program_id(0)

    @pl.when(i % tiles_per_seq == 0)
    def _():
        carry_ref[...] = jnp.zeros(carry_ref.shape, F32)

    row = lax.broadcasted_iota(jnp.int32, (SUBLANES, FF_CHUNK), 0)

    def shifted(h, prev, shift):
        rolled = pltpu.roll(h, shift, 0)
        head = rolled[:SUBLANES]
        for i in range(shift):
            head = jnp.where(row == i, prev[SUBLANES - shift + i:SUBLANES - shift + i + 1], head)
        return jnp.concatenate([head, rolled[SUBLANES:]], axis=0)

    def conv(h, cols, c, half):
        prev = carry_ref[half, c]
        out = (bc_ref[:, cols] + wc_ref[0:1, cols] * shifted(h, prev, 2)
               + wc_ref[1:2, cols] * shifted(h, prev, 1) + wc_ref[2:3, cols] * h)
        carry_ref[half, c] = h[FF_ROWS - SUBLANES:FF_ROWS]
        return out

    for r in range(x_ref.shape[0] // FF_ROWS):
        rows = slice(r * FF_ROWS, (r + 1) * FF_ROWS)
        x = x_ref[rows, :]
        for c in range(N_FF_CHUNKS):
            ucols = slice(c * FF_CHUNK, (c + 1) * FF_CHUNK)
            gcols = slice(D_FF + c * FF_CHUNK, D_FF + (c + 1) * FF_CHUNK)
            u = conv(_dot(x, w_ref[:, ucols]), ucols, c, 0)
            g = conv(_dot(x, w_ref[:, gcols]), gcols, c, 1)
            act = g * u / (1.0 + jnp.exp(-g))
            o_ref[rows, ucols] = act.astype(o_ref.dtype)


def _ffn_up(xn, w, w_conv, b_conv, seq_len):
    m, k = xn.shape
    kern = functools.partial(_ffn_up_kernel, tiles_per_seq=seq_len // ROW_TILE)
    return pl.pallas_call(
        kern,
        grid=(m // ROW_TILE,),
        in_specs=[pl.BlockSpec((ROW_TILE, k), lambda i: (i, 0)),
                  pl.BlockSpec((k, 2 * D_FF), lambda i: (0, 0)),
                  pl.BlockSpec((CONV_WIDTH, 2 * D_FF), lambda i: (0, 0)),
                  pl.BlockSpec((1, 2 * D_FF), lambda i: (0, 0))],
        out_specs=pl.BlockSpec((ROW_TILE, D_FF), lambda i: (i, 0)),
        out_shape=jax.ShapeDtypeStruct((m, D_FF), BF16),
        scratch_shapes=[pltpu.VMEM((2, N_FF_CHUNKS, SUBLANES, FF_CHUNK), F32)],
        compiler_params=_params(("arbitrary",), 56),
        name="ffn_up_conv_gate",
    )(xn, w, w_conv, b_conv)


def kernel(x, attn_norm, ffn_norm, final_norm, fox_w_qkvf, fox_b_f, fox_w_o,
           sb_w_qkv, sb_w_o, ffn_w_up, ffn_w_conv, ffn_b_conv, ffn_w_down):
    b, s, d = x.shape
    m = b * s
    assert d == D_MODEL and s % TQ == 0 and s % ROW_TILE == 0 and s % VT_GROUP == 0 and TQ == TK

    colscale = jnp.concatenate([jnp.full((1, d), ATTN_SCALE * LOG2E, F32),
                                jnp.ones((1, 2 * d), F32)], axis=1)

    hs = x.reshape(m, d)
    xn = _rmsnorm(hs, attn_norm[0], BF16)
    out = None
    for i in range(DEPTH):
        j = i // 2
        if i % 2 == 0:
            w = fox_w_qkvf[j]
            w_gate = jnp.pad(w[:, 3 * d:], ((0, 0), (0, LANES - N_HEADS))).astype(BF16)
            qkv, f_logit = _proj(xn, w[:, :3 * d].astype(BF16), colscale, w_gate)
            b_gate = jnp.pad(fox_b_f[j], (0, LANES - N_HEADS)).reshape(1, LANES)
            gate = _gate_operand(f_logit.reshape(b, s, LANES), b_gate)
            o = _fox_attention(qkv.reshape(b, s, 3 * d), gate)
            w_o = fox_w_o[j]
        else:
            qkv = _proj(xn, sb_w_qkv[j].astype(BF16), colscale)[0]
            o = _sb_attention(qkv.reshape(b, s, 3 * d))
            w_o = sb_w_o[j]
        hs, xn = _mm_resid_norm(o.reshape(m, d), w_o.astype(BF16), hs, ffn_norm[i])
        act = _ffn_up(xn, ffn_w_up[i].astype(BF16), ffn_w_conv[i], ffn_b_conv[i].reshape(1, -1), s)
        w_down = ffn_w_down[i].astype(BF16)
        if i + 1 < DEPTH:
            hs, xn = _mm_resid_norm(act, w_down, hs, attn_norm[i + 1])
        else:
            out = _mm_resid_norm(act, w_down, hs, final_norm, final=True)
    return out.reshape(b, s, d)
```

```python
import functools
import math

import jax
import jax.numpy as jnp
from jax import lax
from jax.experimental import pallas as pl
from jax.experimental.pallas import tpu as pltpu

D_MODEL = 1024
N_HEADS = 16
HEAD_DIM = D_MODEL // N_HEADS
D_FF = 2816
CONV_WIDTH = 3
DEPTH = 4
NORM_EPS = 1e-6
ATTN_SCALE = HEAD_DIM ** -0.5
LOG2E = math.log2(math.e)

F32 = jnp.float32
BF16 = jnp.bfloat16

LANES = 128
SUBLANES = 8
HEADS_PER_PAIR = LANES // HEAD_DIM
N_HEAD_PAIRS = N_HEADS // HEADS_PER_PAIR
ROW_TILE = 512
PROJ_CHUNK = 512
FF_CHUNK = 256
N_FF_CHUNKS = D_FF // FF_CHUNK
FF_ROWS = 128
TQ = 512
TK = 512
VT_BLOCK = 256
VT_GROUP = 1024
FOX_UNROLL = 8
SB_UNROLL = 4
CUM_BLOCK = 128
SB_BLOCK = 256
GATE_PARTS = 3
ONES_LANE = GATE_PARTS * N_HEADS
MIB = 2 ** 20


def _params(semantics, vmem_mib):
    return pltpu.CompilerParams(dimension_semantics=semantics,
                                vmem_limit_bytes=vmem_mib * MIB)


def _dot(a, b):
    return jnp.dot(a, b, preferred_element_type=F32)


def _dot_nt(a, b):
    return lax.dot_general(a, b, (((1,), (1,)), ((), ())), preferred_element_type=F32)


def _split_bf16(x, parts):
    pieces = []
    for _ in range(parts - 1):
        p = x.astype(BF16)
        pieces.append(p)
        x = x - p.astype(F32)
    pieces.append(x.astype(BF16))
    return pieces


def _rms(h, g):
    ms = jnp.mean(h * h, axis=-1, keepdims=True)
    return h * lax.rsqrt(ms + NORM_EPS) * g


def _rmsnorm_kernel(x_ref, g_ref, o_ref):
    o_ref[...] = _rms(x_ref[...], g_ref[...]).astype(o_ref.dtype)


def _rmsnorm(x, g, out_dtype):
    m, d = x.shape
    return pl.pallas_call(
        _rmsnorm_kernel,
        grid=(m // ROW_TILE,),
        in_specs=[pl.BlockSpec((ROW_TILE, d), lambda i: (i, 0)),
                  pl.BlockSpec((1, d), lambda i: (0, 0))],
        out_specs=pl.BlockSpec((ROW_TILE, d), lambda i: (i, 0)),
        out_shape=jax.ShapeDtypeStruct((m, d), out_dtype),
        compiler_params=_params(("parallel",), 32),
        name="rmsnorm",
    )(x, g.reshape(1, d))


def _proj_kernel(x_ref, w_ref, cs_ref, o_ref):
    x = x_ref[...]
    n = w_ref.shape[1]
    for c in range(n // PROJ_CHUNK):
        sl = slice(c * PROJ_CHUNK, (c + 1) * PROJ_CHUNK)
        o_ref[:, sl] = (_dot(x, w_ref[:, sl]) * cs_ref[:, sl]).astype(o_ref.dtype)


def _proj_gate_kernel(x_ref, w_ref, cs_ref, wf_ref, o_ref, f_ref):
    _proj_kernel(x_ref, w_ref, cs_ref, o_ref)
    f_ref[...] = _dot(x_ref[...], wf_ref[...])


def _proj(xn, w, colscale, w_gate=None):
    m, k = xn.shape
    n = w.shape[1]
    in_specs = [pl.BlockSpec((ROW_TILE, k), lambda i: (i, 0)),
                pl.BlockSpec((k, n), lambda i: (0, 0)),
                pl.BlockSpec((1, n), lambda i: (0, 0))]
    out_specs = [pl.BlockSpec((ROW_TILE, n), lambda i: (i, 0))]
    out_shape = [jax.ShapeDtypeStruct((m, n), BF16)]
    args = [xn, w, colscale]
    body = _proj_kernel
    if w_gate is not None:
        in_specs.append(pl.BlockSpec((k, LANES), lambda i: (0, 0)))
        out_specs.append(pl.BlockSpec((ROW_TILE, LANES), lambda i: (i, 0)))
        out_shape.append(jax.ShapeDtypeStruct((m, LANES), F32))
        args.append(w_gate)
        body = _proj_gate_kernel
    return pl.pallas_call(
        body,
        grid=(m // ROW_TILE,),
        in_specs=in_specs,
        out_specs=out_specs,
        out_shape=out_shape,
        compiler_params=_params(("parallel",), 48),
        name="qkv_proj",
    )(*args)


def _softplus2(y):
    neg_abs = pltpu.bitcast(pltpu.bitcast(y, jnp.uint32) | jnp.uint32(0x80000000), F32)
    return jnp.maximum(y, 0.0) + jnp.log(1.0 + jnp.exp2(neg_abs)) * LOG2E


def _gate_cumsum_kernel(f_ref, b_ref, tri_ref, place_ref, o_ref):
    n_blocks = f_ref.shape[1] // CUM_BLOCK
    tri = tri_ref[...]
    lane = lax.broadcasted_iota(jnp.int32, (CUM_BLOCK, LANES), 1)
    ones = jnp.where((lane >= ONES_LANE) & (lane < ONES_LANE + GATE_PARTS), 1.0, 0.0)

    def body(r, carry):
        rows = pl.ds(pl.multiple_of(r * CUM_BLOCK, CUM_BLOCK), CUM_BLOCK)
        logit2 = (f_ref[0, rows, :] + b_ref[...]) * LOG2E
        ls = -_softplus2(-logit2)
        c2 = carry
        for piece in _split_bf16(ls, GATE_PARTS):
            c2 = c2 + _dot(tri, piece)
        operand = ones
        for i, piece in enumerate(_split_bf16(c2, GATE_PARTS)):
            operand = operand + _dot(piece, place_ref[i])
        o_ref[0, rows, :] = operand.astype(o_ref.dtype)
        return c2[CUM_BLOCK - 1:CUM_BLOCK, :]

    lax.fori_loop(0, n_blocks, body, jnp.zeros((1, LANES), F32))


def _gate_operand(f_logit, b_gate):
    b, s, _ = f_logit.shape
    idx = jnp.arange(CUM_BLOCK)
    tri = (idx[None, :] <= idx[:, None]).astype(BF16)
    lane = jnp.arange(LANES)
    place = jnp.stack([((lane[:, None] < N_HEADS) & (lane[None, :] == i * N_HEADS + lane[:, None]))
                       for i in range(GATE_PARTS)]).astype(BF16)
    return pl.pallas_call(
        _gate_cumsum_kernel,
        grid=(b,),
        in_specs=[pl.BlockSpec((1, s, LANES), lambda i: (i, 0, 0)),
                  pl.BlockSpec((1, LANES), lambda i: (0, 0)),
                  pl.BlockSpec((CUM_BLOCK, CUM_BLOCK), lambda i: (0, 0)),
                  pl.BlockSpec((GATE_PARTS, LANES, LANES), lambda i: (0, 0, 0))],
        out_specs=pl.BlockSpec((1, s, LANES), lambda i: (i, 0, 0)),
        out_shape=jax.ShapeDtypeStruct((b, s, LANES), BF16),
        compiler_params=_params(("parallel",), 32),
        name="gate_cumsum",
    )(f_logit, b_gate, tri, place)


def _causal(strict):
    shape = (TK, HEADS_PER_PAIR * TQ)
    key = lax.broadcasted_iota(jnp.int32, shape, 0)
    query = lax.broadcasted_iota(jnp.int32, shape, 1) & (TQ - 1)
    return key < query if strict else key <= query


def _rows(index, size):
    if isinstance(index, int):
        return pl.ds(index * size, size)
    return pl.ds(pl.multiple_of(index * size, size), size)


def _transpose_values(v_ref, store):
    row = lax.broadcasted_iota(jnp.int32, (LANES, LANES), 0)
    col = lax.broadcasted_iota(jnp.int32, (LANES, LANES), 1)
    eye = (row == col).astype(BF16)

    group = VT_GROUP // VT_BLOCK

    def body(i, carry):
        vt = _dot_nt(eye, v_ref[0, _rows(i, VT_GROUP), :]).astype(BF16)
        for g in range(group):
            store(i * group + g, vt[:, g * VT_BLOCK:(g + 1) * VT_BLOCK])
        return carry

    lax.fori_loop(0, v_ref.shape[1] // VT_GROUP, body, 0)


def _value_blocks(vt_ref, j, *lead):
    per_block = TK // VT_BLOCK
    return jnp.concatenate([vt_ref[(*lead, j * per_block + i)] for i in range(per_block)], axis=1)


def _pipelined_sweep(n_tiles, stages, descending, unroll):
    depth = len(stages)
    pairs = [(t, j) for t in range(n_tiles)
             for j in (range(t, -1, -1) if descending else range(t + 1))]
    n = len(pairs)
    assert n >= depth - 1 + unroll and unroll % 2 == 0

    def run(step_pairs, parity, masked, active):
        for s in active:
            slot = parity if s % 2 == 0 else 1 - parity
            tile, block = step_pairs[s]
            if s == 0:
                stages[0](tile, block, slot, masked)
            else:
                stages[s](tile, block, slot)

    def static_step(u):
        active = [s for s in range(depth) if 0 <= u - s < n]
        step_pairs = {s: pairs[u - s] for s in active}
        masked = 0 in step_pairs and step_pairs[0][0] == step_pairs[0][1]
        run(step_pairs, u & 1, masked, active)

    def advance(tile, block):
        if descending:
            last = block == 0
            return jnp.where(last, tile + 1, tile), jnp.where(last, tile + 1, block - 1)
        last = block == tile
        return jnp.where(last, tile + 1, tile), jnp.where(last, 0, block + 1)

    start = depth - 1 + (n - depth + 1) % unroll
    for u in range(start):
        static_step(u)

    def on_diagonal(pair):
        return pair[0] == pair[1]

    mask_cases = sorted({tuple(on_diagonal(pairs[u + i]) for i in range(unroll))
                         for u in range(start, n, unroll)})

    def body(_, carry):
        window = [(carry[2 * s], carry[2 * s + 1]) for s in range(depth)]
        for _ in range(unroll - 1):
            window.insert(0, advance(*window[0]))
        heads = window[:unroll][::-1]
        for masked in mask_cases:
            cond = on_diagonal(heads[0]) == masked[0]
            for i in range(1, unroll):
                cond = jnp.logical_and(cond, on_diagonal(heads[i]) == masked[i])

            @pl.when(cond)
            def _():
                for i in range(unroll):
                    offset = unroll - 1 - i
                    step_pairs = {s: window[offset + s] for s in range(depth)}
                    run(step_pairs, (start + i) & 1, masked[i], range(depth))
        window.insert(0, advance(*window[0]))
        return tuple(v for pair in window[:depth] for v in pair)

    init = ()
    for s in range(depth):
        init += tuple(jnp.int32(v) for v in pairs[start - s])
    lax.fori_loop(0, (n - start) // unroll, body, init)

    for u in range(n, n + depth - 1):
        static_step(u)


def _store_outputs(o_ref, n_tiles, normalized):
    def body(t, carry):
        acc_t = normalized(t)
        out_t = jnp.concatenate([acc_t[:HEAD_DIM, :TQ], acc_t[HEAD_DIM:, TQ:]], axis=0)
        o_ref[0, _rows(t, TQ), :] = out_t.T.astype(o_ref.dtype)
        return carry

    lax.fori_loop(0, n_tiles, body, 0)


def _fox_kernel(q_ref, k_ref, v_ref, gate_ref, perm_ref, o_ref,
                vt_ref, qa_ref, m_ref, acc_ref, s_buf, m_buf, a_buf):
    pair = pl.program_id(1)
    n_tiles = q_ref.shape[1] // TQ

    row = lax.broadcasted_iota(jnp.int32, (LANES, VT_BLOCK), 0)

    def store(j, vt):
        one = jnp.ones_like(vt)
        vt_ref[0, j] = jnp.where(row < HEAD_DIM, vt, one)
        vt_ref[1, j] = jnp.where(row >= HEAD_DIM, vt, one)

    _transpose_values(v_ref, store)

    q = q_ref[0]
    gate_all = gate_ref[0]
    zero = jnp.zeros_like(q)
    lane = lax.broadcasted_iota(jnp.int32, q.shape, 1)
    lane_row = lax.broadcasted_iota(jnp.int32, (1, LANES), 1)
    for hh in range(HEADS_PER_PAIR):
        h = HEADS_PER_PAIR * pair + hh
        minus = jnp.zeros((1, LANES), F32)
        for i in range(GATE_PARTS):
            minus = jnp.where(lane_row == i * N_HEADS + h, -1.0, minus)
        gate_lanes = (_dot(gate_all, perm_ref[h]) + minus).astype(BF16)
        feat = (lane < HEAD_DIM) if hh == 0 else (lane >= HEAD_DIM)
        operand = jnp.concatenate([jnp.where(feat, q, zero), gate_lanes], axis=1)
        qa_ref[:, hh * TQ:(hh + 1) * TQ, :] = operand.reshape(n_tiles, TQ, 2 * LANES)
    m_ref[...] = jnp.full(m_ref.shape, -jnp.inf, F32)
    acc_ref[...] = jnp.zeros(acc_ref.shape, F32)

    def scores(t, j, slot, masked):
        rows = _rows(j, TK)
        k_aug = jnp.concatenate([k_ref[0, rows, :], gate_ref[0, rows, :]], axis=1)
        s = _dot_nt(k_aug, qa_ref[t])
        if masked:
            s = jnp.where(_causal(strict=False), s, -jnp.inf)
        m_old = m_ref[t]
        m_new = jnp.maximum(m_old, jnp.max(s, axis=0, keepdims=True))
        m_ref[t] = m_new
        m_buf[slot] = m_new
        a_buf[slot] = jnp.exp2(m_old - m_new)
        s_buf[slot] = s

    def accumulate(t, j, slot):
        prob = jnp.exp2(s_buf[slot] - m_buf[slot]).astype(BF16)
        pv = jnp.concatenate(
            [_dot(_value_blocks(vt_ref, j, hh), prob[:, hh * TQ:(hh + 1) * TQ])
             for hh in range(HEADS_PER_PAIR)], axis=1)
        acc_ref[t] = acc_ref[t] * a_buf[slot] + pv

    _pipelined_sweep(n_tiles, (scores, accumulate), descending=False, unroll=FOX_UNROLL)

    def normalized(t):
        acc = acc_ref[t]
        denom = jnp.concatenate(
            [jnp.broadcast_to(acc[HEAD_DIM:HEAD_DIM + 1, :TQ], (LANES, TQ)),
             jnp.broadcast_to(acc[0:1, TQ:], (LANES, TQ))], axis=1)
        return acc / denom

    _store_outputs(o_ref, n_tiles, normalized)


def _sb_kernel(q_ref, k_ref, v_ref, tt_ref, o_ref,
               vt_ref, qs_ref, r_ref, acc_ref, z_buf, w_buf):
    n_tiles = q_ref.shape[1] // TQ

    def store(j, vt):
        vt_ref[j] = vt

    _transpose_values(v_ref, store)

    q = q_ref[0]
    zero = jnp.zeros_like(q)
    lane = lax.broadcasted_iota(jnp.int32, q.shape, 1)
    for hh in range(HEADS_PER_PAIR):
        feat = (lane < HEAD_DIM) if hh == 0 else (lane >= HEAD_DIM)
        qs_ref[:, hh * TQ:(hh + 1) * TQ, :] = jnp.where(feat, q, zero).reshape(n_tiles, TQ, LANES)
    r_ref[...] = jnp.zeros(r_ref.shape, F32)
    acc_ref[...] = jnp.zeros(acc_ref.shape, F32)

    n_sub = TK // SB_BLOCK

    def logits(t, j, slot, masked):
        z = _dot_nt(k_ref[0, _rows(j, TK), :], qs_ref[t])
        if masked:
            z = jnp.where(_causal(strict=True), z, -jnp.inf)
        z_buf[slot] = z

    def weights(t, j, slot):
        r = r_ref[t]
        for g in reversed(range(n_sub)):
            rows = slice(g * SB_BLOCK, (g + 1) * SB_BLOCK)
            z = z_buf[slot, rows, :]
            sp = _softplus2(z)
            sums = _dot(tt_ref[...], sp.astype(BF16))
            w_buf[slot, rows, :] = jnp.exp2(z - sp - sums[:SB_BLOCK] - r).astype(BF16)
            r = r + sums[SB_BLOCK:SB_BLOCK + 1]
        r_ref[t] = r

    def accumulate(t, j, slot):
        acc_ref[t] += _dot(_value_blocks(vt_ref, j), w_buf[slot])

    _pipelined_sweep(n_tiles, (logits, weights, accumulate), descending=True, unroll=SB_UNROLL)
    _store_outputs(o_ref, n_tiles, lambda t: acc_ref[t])


def _attention_call(body, name, qkv, extra_inputs, extra_specs, scratch_shapes):
    b, s, _ = qkv.shape

    def column_block(offset):
        return pl.BlockSpec((1, s, LANES), lambda b, p: (b, 0, offset + p))

    return pl.pallas_call(
        body,
        grid=(b, N_HEAD_PAIRS),
        in_specs=[column_block(0), column_block(N_HEAD_PAIRS), column_block(2 * N_HEAD_PAIRS)
                  ] + extra_specs,
        out_specs=column_block(0),
        out_shape=jax.ShapeDtypeStruct((b, s, D_MODEL), BF16),
        scratch_shapes=scratch_shapes,
        compiler_params=_params(("parallel", "parallel"), 56),
        name=name,
    )(qkv, qkv, qkv, *extra_inputs)


def _fox_attention(qkv, gate):
    b, s, _ = qkv.shape
    n_tiles = s // TQ
    lane = jnp.arange(LANES)
    perm = jnp.stack([
        sum(((lane[:, None] == i * N_HEADS + h) & (lane[None, :] == ONES_LANE + i))
            for i in range(GATE_PARTS))
        for h in range(N_HEADS)]).astype(BF16)
    cols = HEADS_PER_PAIR * TQ
    return _attention_call(
        _fox_kernel, "fox_attention", qkv, [gate, perm],
        [pl.BlockSpec((1, s, LANES), lambda b, p: (b, 0, 0)),
         pl.BlockSpec((N_HEADS, LANES, LANES), lambda b, p: (0, 0, 0))],
        [pltpu.VMEM((HEADS_PER_PAIR, s // VT_BLOCK, LANES, VT_BLOCK), BF16),
         pltpu.VMEM((n_tiles, cols, 2 * LANES), BF16),
         pltpu.VMEM((n_tiles, 1, cols), F32),
         pltpu.VMEM((n_tiles, LANES, cols), F32),
         pltpu.VMEM((2, TK, cols), F32),
         pltpu.VMEM((2, 1, cols), F32),
         pltpu.VMEM((2, 1, cols), F32)])


def _sb_attention(qkv):
    b, s, _ = qkv.shape
    n_tiles = s // TQ
    idx = jnp.arange(SB_BLOCK)
    later = (idx[None, :] > idx[:, None]).astype(BF16)
    tt = jnp.concatenate([later, jnp.ones((SUBLANES, SB_BLOCK), BF16)], axis=0)
    cols = HEADS_PER_PAIR * TQ
    return _attention_call(
        _sb_kernel, "sb_attention", qkv, [tt],
        [pl.BlockSpec((SB_BLOCK + SUBLANES, SB_BLOCK), lambda b, p: (0, 0))],
        [pltpu.VMEM((s // VT_BLOCK, LANES, VT_BLOCK), BF16),
         pltpu.VMEM((n_tiles, cols, LANES), BF16),
         pltpu.VMEM((n_tiles, 1, cols), F32),
         pltpu.VMEM((n_tiles, LANES, cols), F32),
         pltpu.VMEM((2, TK, cols), F32),
         pltpu.VMEM((2, TK, cols), BF16)])


def _mm_resid_norm_kernel(a_ref, w_ref, h_ref, g_ref, hs_ref, xn_ref):
    h = _dot(a_ref[...], w_ref[...]) + h_ref[...]
    hs_ref[...] = h
    xn_ref[...] = _rms(h, g_ref[...]).astype(xn_ref.dtype)


def _mm_resid_final_kernel(a_ref, w_ref, h_ref, g_ref, o_ref):
    h = _dot(a_ref[...], w_ref[...]) + h_ref[...]
    o_ref[...] = _rms(h, g_ref[...]).astype(o_ref.dtype)


def _mm_resid_norm(a, w, hs, g, final=False):
    m, k = a.shape
    d = w.shape[1]
    row = lambda i: (i, 0)
    in_specs = [pl.BlockSpec((ROW_TILE, k), row),
                pl.BlockSpec((k, d), lambda i: (0, 0)),
                pl.BlockSpec((ROW_TILE, d), row),
                pl.BlockSpec((1, d), lambda i: (0, 0))]
    if final:
        body = _mm_resid_final_kernel
        out_specs = pl.BlockSpec((ROW_TILE, d), row)
        out_shape = jax.ShapeDtypeStruct((m, d), F32)
    else:
        body = _mm_resid_norm_kernel
        out_specs = [pl.BlockSpec((ROW_TILE, d), row), pl.BlockSpec((ROW_TILE, d), row)]
        out_shape = [jax.ShapeDtypeStruct((m, d), F32), jax.ShapeDtypeStruct((m, d), BF16)]
    return pl.pallas_call(
        body,
        grid=(m // ROW_TILE,),
        in_specs=in_specs,
        out_specs=out_specs,
        out_shape=out_shape,
        compiler_params=_params(("parallel",), 48),
        name="matmul_residual_norm",
    )(a, w, hs, g.reshape(1, d))


def _ffn_up_kernel(x_ref, w_ref, wc_ref, bc_ref, o_ref, carry_ref, *, tiles_per_seq):
    i = pl.program_id(0)

    @pl.when(i % tiles_per_seq == 0)
    def _():
        carry_ref[...] = jnp.zeros(carry_ref.shape, F32)

    row = lax.broadcasted_iota(jnp.int32, (SUBLANES, FF_CHUNK), 0)

    def shifted(h, prev, shift):
        rolled = pltpu.roll(h, shift, 0)
        head = rolled[:SUBLANES]
        for i in range(shift):
            head = jnp.where(row == i, prev[SUBLANES - shift + i:SUBLANES - shift + i + 1], head)
        return jnp.concatenate([head, rolled[SUBLANES:]], axis=0)

    def conv(h, cols, c, half):
        prev = carry_ref[half, c]
        out = (bc_ref[:, cols] + wc_ref[0:1, cols] * shifted(h, prev, 2)
               + wc_ref[1:2, cols] * shifted(h, prev, 1) + wc_ref[2:3, cols] * h)
        carry_ref[half, c] = h[FF_ROWS - SUBLANES:FF_ROWS]
        return out

    for r in range(x_ref.shape[0] // FF_ROWS):
        rows = slice(r * FF_ROWS, (r + 1) * FF_ROWS)
        x = x_ref[rows, :]
        for c in range(N_FF_CHUNKS):
            ucols = slice(c * FF_CHUNK, (c + 1) * FF_CHUNK)
            gcols = slice(D_FF + c * FF_CHUNK, D_FF + (c + 1) * FF_CHUNK)
            u = conv(_dot(x, w_ref[:, ucols]), ucols, c, 0)
            g = conv(_dot(x, w_ref[:, gcols]), gcols, c, 1)
            act = g * u / (1.0 + jnp.exp(-g))
            o_ref[rows, ucols] = act.astype(o_ref.dtype)


def _ffn_up(xn, w, w_conv, b_conv, seq_len):
    m, k = xn.shape
    kern = functools.partial(_ffn_up_kernel, tiles_per_seq=seq_len // ROW_TILE)
    return pl.pallas_call(
        kern,
        grid=(m // ROW_TILE,),
        in_specs=[pl.BlockSpec((ROW_TILE, k), lambda i: (i, 0)),
                  pl.BlockSpec((k, 2 * D_FF), lambda i: (0, 0)),
                  pl.BlockSpec((CONV_WIDTH, 2 * D_FF), lambda i: (0, 0)),
                  pl.BlockSpec((1, 2 * D_FF), lambda i: (0, 0))],
        out_specs=pl.BlockSpec((ROW_TILE, D_FF), lambda i: (i, 0)),
        out_shape=jax.ShapeDtypeStruct((m, D_FF), BF16),
        scratch_shapes=[pltpu.VMEM((2, N_FF_CHUNKS, SUBLANES, FF_CHUNK), F32)],
        compiler_params=_params(("arbitrary",), 56),
        name="ffn_up_conv_gate",
    )(xn, w, w_conv, b_conv)


def kernel(x, attn_norm, ffn_norm, final_norm, fox_w_qkvf, fox_b_f, fox_w_o,
           sb_w_qkv, sb_w_o, ffn_w_up, ffn_w_conv, ffn_b_conv, ffn_w_down):
    b, s, d = x.shape
    m = b * s
    assert d == D_MODEL and s % TQ == 0 and s % ROW_TILE == 0 and s % VT_GROUP == 0 and TQ == TK

    colscale = jnp.concatenate([jnp.full((1, d), ATTN_SCALE * LOG2E, F32),
                                jnp.ones((1, 2 * d), F32)], axis=1)

    hs = x.reshape(m, d)
    xn = _rmsnorm(hs, attn_norm[0], BF16)
    out = None
    for i in range(DEPTH):
        j = i // 2
        if i % 2 == 0:
            w = fox_w_qkvf[j]
            w_gate = jnp.pad(w[:, 3 * d:], ((0, 0), (0, LANES - N_HEADS))).astype(BF16)
            qkv, f_logit = _proj(xn, w[:, :3 * d].astype(BF16), colscale, w_gate)
            b_gate = jnp.pad(fox_b_f[j], (0, LANES - N_HEADS)).reshape(1, LANES)
            gate = _gate_operand(f_logit.reshape(b, s, LANES), b_gate)
            o = _fox_attention(qkv.reshape(b, s, 3 * d), gate)
            w_o = fox_w_o[j]
        else:
            qkv = _proj(xn, sb_w_qkv[j].astype(BF16), colscale)[0]
            o = _sb_attention(qkv.reshape(b, s, 3 * d))
            w_o = sb_w_o[j]
        hs, xn = _mm_resid_norm(o.reshape(m, d), w_o.astype(BF16), hs, ffn_norm[i])
        act = _ffn_up(xn, ffn_w_up[i].astype(BF16), ffn_w_conv[i], ffn_b_conv[i].reshape(1, -1), s)
        w_down = ffn_w_down[i].astype(BF16)
        if i + 1 < DEPTH:
            hs, xn = _mm_resid_norm(act, w_down, hs, attn_norm[i + 1])
        else:
            out = _mm_resid_norm(act, w_down, hs, final_norm, final=True)
    return out.reshape(b, s, d)
```

```python
import functools
import math

import jax
import jax.numpy as jnp
from jax import lax
from jax.experimental import pallas as pl
from jax.experimental.pallas import tpu as pltpu

D_MODEL = 1024
N_HEADS = 16
HEAD_DIM = D_MODEL // N_HEADS
D_FF = 2816
CONV_WIDTH = 3
DEPTH = 4
NORM_EPS = 1e-6
ATTN_SCALE = HEAD_DIM ** -0.5
LOG2E = math.log2(math.e)

F32 = jnp.float32
BF16 = jnp.bfloat16

LANES = 128
SUBLANES = 8
HEADS_PER_PAIR = LANES // HEAD_DIM
N_HEAD_PAIRS = N_HEADS // HEADS_PER_PAIR
ROW_TILE = 512
PROJ_CHUNK = 512
FF_CHUNK = 256
N_FF_CHUNKS = D_FF // FF_CHUNK
FF_ROWS = 128
TQ = 512
TK = 512
VT_BLOCK = 256
VT_GROUP = 1024
FOX_UNROLL = 4
SB_UNROLL = 4
CUM_BLOCK = 128
SB_BLOCK = 256
GATE_PARTS = 3
ONES_LANE = GATE_PARTS * N_HEADS
MIB = 2 ** 20


def _params(semantics, vmem_mib):
    return pltpu.CompilerParams(dimension_semantics=semantics,
                                vmem_limit_bytes=vmem_mib * MIB)


def _dot(a, b):
    return jnp.dot(a, b, preferred_element_type=F32)


def _dot_nt(a, b):
    return lax.dot_general(a, b, (((1,), (1,)), ((), ())), preferred_element_type=F32)


def _split_bf16(x, parts):
    pieces = []
    for _ in range(parts - 1):
        p = x.astype(BF16)
        pieces.append(p)
        x = x - p.astype(F32)
    pieces.append(x.astype(BF16))
    return pieces


def _rms(h, g):
    ms = jnp.mean(h * h, axis=-1, keepdims=True)
    return h * lax.rsqrt(ms + NORM_EPS) * g


def _rmsnorm_kernel(x_ref, g_ref, o_ref):
    o_ref[...] = _rms(x_ref[...], g_ref[...]).astype(o_ref.dtype)


def _rmsnorm(x, g, out_dtype):
    m, d = x.shape
    return pl.pallas_call(
        _rmsnorm_kernel,
        grid=(m // ROW_TILE,),
        in_specs=[pl.BlockSpec((ROW_TILE, d), lambda i: (i, 0)),
                  pl.BlockSpec((1, d), lambda i: (0, 0))],
        out_specs=pl.BlockSpec((ROW_TILE, d), lambda i: (i, 0)),
        out_shape=jax.ShapeDtypeStruct((m, d), out_dtype),
        compiler_params=_params(("parallel",), 32),
        name="rmsnorm",
    )(x, g.reshape(1, d))


def _proj_kernel(x_ref, w_ref, cs_ref, o_ref):
    x = x_ref[...]
    n = w_ref.shape[1]
    for c in range(n // PROJ_CHUNK):
        sl = slice(c * PROJ_CHUNK, (c + 1) * PROJ_CHUNK)
        o_ref[:, sl] = (_dot(x, w_ref[:, sl]) * cs_ref[:, sl]).astype(o_ref.dtype)


def _proj_gate_kernel(x_ref, w_ref, cs_ref, wf_ref, o_ref, f_ref):
    _proj_kernel(x_ref, w_ref, cs_ref, o_ref)
    f_ref[...] = _dot(x_ref[...], wf_ref[...])


def _proj(xn, w, colscale, w_gate=None):
    m, k = xn.shape
    n = w.shape[1]
    in_specs = [pl.BlockSpec((ROW_TILE, k), lambda i: (i, 0)),
                pl.BlockSpec((k, n), lambda i: (0, 0)),
                pl.BlockSpec((1, n), lambda i: (0, 0))]
    out_specs = [pl.BlockSpec((ROW_TILE, n), lambda i: (i, 0))]
    out_shape = [jax.ShapeDtypeStruct((m, n), BF16)]
    args = [xn, w, colscale]
    body = _proj_kernel
    if w_gate is not None:
        in_specs.append(pl.BlockSpec((k, LANES), lambda i: (0, 0)))
        out_specs.append(pl.BlockSpec((ROW_TILE, LANES), lambda i: (i, 0)))
        out_shape.append(jax.ShapeDtypeStruct((m, LANES), F32))
        args.append(w_gate)
        body = _proj_gate_kernel
    return pl.pallas_call(
        body,
        grid=(m // ROW_TILE,),
        in_specs=in_specs,
        out_specs=out_specs,
        out_shape=out_shape,
        compiler_params=_params(("parallel",), 48),
        name="qkv_proj",
    )(*args)


def _softplus2(y):
    neg_abs = pltpu.bitcast(pltpu.bitcast(y, jnp.uint32) | jnp.uint32(0x80000000), F32)
    return jnp.maximum(y, 0.0) + jnp.log(1.0 + jnp.exp2(neg_abs)) * LOG2E


def _gate_cumsum_kernel(f_ref, b_ref, tri_ref, place_ref, o_ref):
    n_blocks = f_ref.shape[1] // CUM_BLOCK
    tri = tri_ref[...]
    lane = lax.broadcasted_iota(jnp.int32, (CUM_BLOCK, LANES), 1)
    ones = jnp.where((lane >= ONES_LANE) & (lane < ONES_LANE + GATE_PARTS), 1.0, 0.0)

    def body(r, carry):
        rows = pl.ds(pl.multiple_of(r * CUM_BLOCK, CUM_BLOCK), CUM_BLOCK)
        logit2 = (f_ref[0, rows, :] + b_ref[...]) * LOG2E
        ls = -_softplus2(-logit2)
        c2 = carry
        for piece in _split_bf16(ls, GATE_PARTS):
            c2 = c2 + _dot(tri, piece)
        operand = ones
        for i, piece in enumerate(_split_bf16(c2, GATE_PARTS)):
            operand = operand + _dot(piece, place_ref[i])
        o_ref[0, rows, :] = operand.astype(o_ref.dtype)
        return c2[CUM_BLOCK - 1:CUM_BLOCK, :]

    lax.fori_loop(0, n_blocks, body, jnp.zeros((1, LANES), F32))


def _gate_operand(f_logit, b_gate):
    b, s, _ = f_logit.shape
    idx = jnp.arange(CUM_BLOCK)
    tri = (idx[None, :] <= idx[:, None]).astype(BF16)
    lane = jnp.arange(LANES)
    place = jnp.stack([((lane[:, None] < N_HEADS) & (lane[None, :] == i * N_HEADS + lane[:, None]))
                       for i in range(GATE_PARTS)]).astype(BF16)
    return pl.pallas_call(
        _gate_cumsum_kernel,
        grid=(b,),
        in_specs=[pl.BlockSpec((1, s, LANES), lambda i: (i, 0, 0)),
                  pl.BlockSpec((1, LANES), lambda i: (0, 0)),
                  pl.BlockSpec((CUM_BLOCK, CUM_BLOCK), lambda i: (0, 0)),
                  pl.BlockSpec((GATE_PARTS, LANES, LANES), lambda i: (0, 0, 0))],
        out_specs=pl.BlockSpec((1, s, LANES), lambda i: (i, 0, 0)),
        out_shape=jax.ShapeDtypeStruct((b, s, LANES), BF16),
        compiler_params=_params(("parallel",), 32),
        name="gate_cumsum",
    )(f_logit, b_gate, tri, place)


def _causal(strict):
    shape = (TK, HEADS_PER_PAIR * TQ)
    key = lax.broadcasted_iota(jnp.int32, shape, 0)
    query = lax.broadcasted_iota(jnp.int32, shape, 1) & (TQ - 1)
    return key < query if strict else key <= query


def _rows(index, size):
    if isinstance(index, int):
        return pl.ds(index * size, size)
    return pl.ds(pl.multiple_of(index * size, size), size)


def _transpose_values(v_ref, store):
    row = lax.broadcasted_iota(jnp.int32, (LANES, LANES), 0)
    col = lax.broadcasted_iota(jnp.int32, (LANES, LANES), 1)
    eye = (row == col).astype(BF16)

    group = VT_GROUP // VT_BLOCK

    def body(i, carry):
        vt = _dot_nt(eye, v_ref[0, _rows(i, VT_GROUP), :]).astype(BF16)
        for g in range(group):
            store(i * group + g, vt[:, g * VT_BLOCK:(g + 1) * VT_BLOCK])
        return carry

    lax.fori_loop(0, v_ref.shape[1] // VT_GROUP, body, 0)


def _value_blocks(vt_ref, j, *lead):
    per_block = TK // VT_BLOCK
    return jnp.concatenate([vt_ref[(*lead, j * per_block + i)] for i in range(per_block)], axis=1)


def _pipelined_sweep(n_tiles, stages, descending, unroll, diagonal_aware=1):
    depth = len(stages)
    pairs = [(t, j) for t in range(n_tiles)
             for j in (range(t, -1, -1) if descending else range(t + 1))]
    n = len(pairs)
    assert n >= depth - 1 + unroll and unroll % 2 == 0

    def on_diagonal(pair):
        return pair[0] == pair[1]

    def run(step_pairs, parity, diagonal, active):
        for s in active:
            slot = parity if s % 2 == 0 else 1 - parity
            tile, block = step_pairs[s]
            if s < diagonal_aware:
                stages[s](tile, block, slot, diagonal[s])
            else:
                stages[s](tile, block, slot)

    def static_step(u):
        active = [s for s in range(depth) if 0 <= u - s < n]
        step_pairs = {s: pairs[u - s] for s in active}
        run(step_pairs, u & 1, {s: on_diagonal(step_pairs[s]) for s in active}, active)

    def advance(tile, block):
        if descending:
            last = block == 0
            return jnp.where(last, tile + 1, tile), jnp.where(last, tile + 1, block - 1)
        last = block == tile
        return jnp.where(last, tile + 1, tile), jnp.where(last, 0, block + 1)

    start = depth - 1 + (n - depth + 1) % unroll
    for u in range(start):
        static_step(u)

    span = unroll + diagonal_aware - 1
    cases = sorted({tuple(on_diagonal(pairs[u - diagonal_aware + 1 + i]) for i in range(span))
                    for u in range(start, n, unroll)})

    def body(_, carry):
        window = [(carry[2 * s], carry[2 * s + 1]) for s in range(depth)]
        for _ in range(unroll - 1):
            window.insert(0, advance(*window[0]))
        for case in cases:
            cond = on_diagonal(window[0]) == case[span - 1]
            for k in range(1, span):
                cond = jnp.logical_and(cond, on_diagonal(window[k]) == case[span - 1 - k])

            @pl.when(cond)
            def _():
                for i in range(unroll):
                    offset = unroll - 1 - i
                    step_pairs = {s: window[offset + s] for s in range(depth)}
                    diagonal = {s: case[diagonal_aware - 1 + i - s] for s in range(diagonal_aware)}
                    run(step_pairs, (start + i) & 1, diagonal, range(depth))
        window.insert(0, advance(*window[0]))
        return tuple(v for pair in window[:depth] for v in pair)

    init = ()
    for s in range(depth):
        init += tuple(jnp.int32(v) for v in pairs[start - s])
    lax.fori_loop(0, (n - start) // unroll, body, init)

    for u in range(n, n + depth - 1):
        static_step(u)


def _store_outputs(o_ref, n_tiles, normalized):
    def body(t, carry):
        acc_t = normalized(t)
        out_t = jnp.concatenate([acc_t[:HEAD_DIM, :TQ], acc_t[HEAD_DIM:, TQ:]], axis=0)
        o_ref[0, _rows(t, TQ), :] = out_t.T.astype(o_ref.dtype)
        return carry

    lax.fori_loop(0, n_tiles, body, 0)


def _fox_kernel(q_ref, k_ref, v_ref, gate_ref, perm_ref, o_ref,
                vt_ref, qa_ref, m_ref, acc_ref, s_buf, m_buf, a_buf):
    pair = pl.program_id(1)
    n_tiles = q_ref.shape[1] // TQ

    row = lax.broadcasted_iota(jnp.int32, (LANES, VT_BLOCK), 0)

    def store(j, vt):
        one = jnp.ones_like(vt)
        vt_ref[0, j] = jnp.where(row < HEAD_DIM, vt, one)
        vt_ref[1, j] = jnp.where(row >= HEAD_DIM, vt, one)

    _transpose_values(v_ref, store)

    q = q_ref[0]
    gate_all = gate_ref[0]
    zero = jnp.zeros_like(q)
    lane = lax.broadcasted_iota(jnp.int32, q.shape, 1)
    lane_row = lax.broadcasted_iota(jnp.int32, (1, LANES), 1)
    for hh in range(HEADS_PER_PAIR):
        h = HEADS_PER_PAIR * pair + hh
        minus = jnp.zeros((1, LANES), F32)
        for i in range(GATE_PARTS):
            minus = jnp.where(lane_row == i * N_HEADS + h, -1.0, minus)
        gate_lanes = (_dot(gate_all, perm_ref[h]) + minus).astype(BF16)
        feat = (lane < HEAD_DIM) if hh == 0 else (lane >= HEAD_DIM)
        operand = jnp.concatenate([jnp.where(feat, q, zero), gate_lanes], axis=1)
        qa_ref[:, hh * TQ:(hh + 1) * TQ, :] = operand.reshape(n_tiles, TQ, 2 * LANES)
    m_ref[...] = jnp.full(m_ref.shape, -jnp.inf, F32)
    acc_ref[...] = jnp.zeros(acc_ref.shape, F32)

    def scores(t, j, slot, masked):
        rows = _rows(j, TK)
        k_aug = jnp.concatenate([k_ref[0, rows, :], gate_ref[0, rows, :]], axis=1)
        s = _dot_nt(k_aug, qa_ref[t])
        if masked:
            s = jnp.where(_causal(strict=False), s, -jnp.inf)
        m_old = m_ref[t]
        m_new = jnp.maximum(m_old, jnp.max(s, axis=0, keepdims=True))
        m_ref[t] = m_new
        m_buf[slot] = m_new
        a_buf[slot] = jnp.exp2(m_old - m_new)
        s_buf[slot] = s

    def accumulate(t, j, slot):
        prob = jnp.exp2(s_buf[slot] - m_buf[slot]).astype(BF16)
        pv = jnp.concatenate(
            [_dot(_value_blocks(vt_ref, j, hh), prob[:, hh * TQ:(hh + 1) * TQ])
             for hh in range(HEADS_PER_PAIR)], axis=1)
        acc_ref[t] = acc_ref[t] * a_buf[slot] + pv

    _pipelined_sweep(n_tiles, (scores, accumulate), descending=False, unroll=FOX_UNROLL)

    def normalized(t):
        acc = acc_ref[t]
        denom = jnp.concatenate(
            [jnp.broadcast_to(acc[HEAD_DIM:HEAD_DIM + 1, :TQ], (LANES, TQ)),
             jnp.broadcast_to(acc[0:1, TQ:], (LANES, TQ))], axis=1)
        return acc / denom

    _store_outputs(o_ref, n_tiles, normalized)


def _sb_kernel(q_ref, k_ref, v_ref, tt_ref, o_ref,
               vt_ref, qs_ref, r_ref, acc_ref, z_buf, w_buf):
    n_tiles = q_ref.shape[1] // TQ

    def store(j, vt):
        vt_ref[j] = vt

    _transpose_values(v_ref, store)

    q = q_ref[0]
    zero = jnp.zeros_like(q)
    lane = lax.broadcasted_iota(jnp.int32, q.shape, 1)
    for hh in range(HEADS_PER_PAIR):
        feat = (lane < HEAD_DIM) if hh == 0 else (lane >= HEAD_DIM)
        qs_ref[:, hh * TQ:(hh + 1) * TQ, :] = jnp.where(feat, q, zero).reshape(n_tiles, TQ, LANES)
    r_ref[...] = jnp.zeros(r_ref.shape, F32)
    acc_ref[...] = jnp.zeros(acc_ref.shape, F32)

    n_sub = TK // SB_BLOCK

    def logits(t, j, slot, masked):
        z = _dot_nt(k_ref[0, _rows(j, TK), :], qs_ref[t])
        if masked:
            z = jnp.where(_causal(strict=True), z, -jnp.inf)
        z_buf[slot] = z

    def weights(t, j, slot, diagonal):
        r = r_ref[t]
        for g in reversed(range(n_sub)):
            rows = slice(g * SB_BLOCK, (g + 1) * SB_BLOCK)
            dead = g * SB_BLOCK if diagonal else 0
            live = [slice(h * TQ + dead, (h + 1) * TQ) for h in range(HEADS_PER_PAIR)]
            z = jnp.concatenate([z_buf[slot, rows, c] for c in live], axis=1)
            r_live = jnp.concatenate([r[:, c] for c in live], axis=1)
            sp = _softplus2(z)
            sums = _dot(tt_ref[...], sp.astype(BF16))
            w = jnp.exp2(z - sp - sums[:SB_BLOCK] - r_live).astype(BF16)
            r_live = r_live + sums[SB_BLOCK:SB_BLOCK + 1]
            width = TQ - dead
            pieces = []
            for h, c in enumerate(live):
                w_buf[slot, rows, c] = w[:, h * width:(h + 1) * width]
                if dead:
                    w_buf[slot, rows, h * TQ:h * TQ + dead] = jnp.zeros((SB_BLOCK, dead), BF16)
                    pieces.append(r[:, h * TQ:h * TQ + dead])
                pieces.append(r_live[:, h * width:(h + 1) * width])
            r = jnp.concatenate(pieces, axis=1)
        r_ref[t] = r

    def accumulate(t, j, slot):
        acc_ref[t] += _dot(_value_blocks(vt_ref, j), w_buf[slot])

    _pipelined_sweep(n_tiles, (logits, weights, accumulate), descending=True, unroll=SB_UNROLL,
                     diagonal_aware=2)
    _store_outputs(o_ref, n_tiles, lambda t: acc_ref[t])


def _attention_call(body, name, qkv, extra_inputs, extra_specs, scratch_shapes):
    b, s, _ = qkv.shape

    def column_block(offset):
        return pl.BlockSpec((1, s, LANES), lambda b, p: (b, 0, offset + p))

    return pl.pallas_call(
        body,
        grid=(b, N_HEAD_PAIRS),
        in_specs=[column_block(0), column_block(N_HEAD_PAIRS), column_block(2 * N_HEAD_PAIRS)
                  ] + extra_specs,
        out_specs=column_block(0),
        out_shape=jax.ShapeDtypeStruct((b, s, D_MODEL), BF16),
        scratch_shapes=scratch_shapes,
        compiler_params=_params(("parallel", "parallel"), 56),
        name=name,
    )(qkv, qkv, qkv, *extra_inputs)


def _fox_attention(qkv, gate):
    b, s, _ = qkv.shape
    n_tiles = s // TQ
    lane = jnp.arange(LANES)
    perm = jnp.stack([
        sum(((lane[:, None] == i * N_HEADS + h) & (lane[None, :] == ONES_LANE + i))
            for i in range(GATE_PARTS))
        for h in range(N_HEADS)]).astype(BF16)
    cols = HEADS_PER_PAIR * TQ
    return _attention_call(
        _fox_kernel, "fox_attention", qkv, [gate, perm],
        [pl.BlockSpec((1, s, LANES), lambda b, p: (b, 0, 0)),
         pl.BlockSpec((N_HEADS, LANES, LANES), lambda b, p: (0, 0, 0))],
        [pltpu.VMEM((HEADS_PER_PAIR, s // VT_BLOCK, LANES, VT_BLOCK), BF16),
         pltpu.VMEM((n_tiles, cols, 2 * LANES), BF16),
         pltpu.VMEM((n_tiles, 1, cols), F32),
         pltpu.VMEM((n_tiles, LANES, cols), F32),
         pltpu.VMEM((2, TK, cols), F32),
         pltpu.VMEM((2, 1, cols), F32),
         pltpu.VMEM((2, 1, cols), F32)])


def _sb_attention(qkv):
    b, s, _ = qkv.shape
    n_tiles = s // TQ
    idx = jnp.arange(SB_BLOCK)
    later = (idx[None, :] > idx[:, None]).astype(BF16)
    tt = jnp.concatenate([later, jnp.ones((SUBLANES, SB_BLOCK), BF16)], axis=0)
    cols = HEADS_PER_PAIR * TQ
    return _attention_call(
        _sb_kernel, "sb_attention", qkv, [tt],
        [pl.BlockSpec((SB_BLOCK + SUBLANES, SB_BLOCK), lambda b, p: (0, 0))],
        [pltpu.VMEM((s // VT_BLOCK, LANES, VT_BLOCK), BF16),
         pltpu.VMEM((n_tiles, cols, LANES), BF16),
         pltpu.VMEM((n_tiles, 1, cols), F32),
         pltpu.VMEM((n_tiles, LANES, cols), F32),
         pltpu.VMEM((2, TK, cols), F32),
         pltpu.VMEM((2, TK, cols), BF16)])


def _mm_resid_norm_kernel(a_ref, w_ref, h_ref, g_ref, hs_ref, xn_ref):
    h = _dot(a_ref[...], w_ref[...]) + h_ref[...]
    hs_ref[...] = h
    xn_ref[...] = _rms(h, g_ref[...]).astype(xn_ref.dtype)


def _mm_resid_final_kernel(a_ref, w_ref, h_ref, g_ref, o_ref):
    h = _dot(a_ref[...], w_ref[...]) + h_ref[...]
    o_ref[...] = _rms(h, g_ref[...]).astype(o_ref.dtype)


def _mm_resid_norm(a, w, hs, g, final=False):
    m, k = a.shape
    d = w.shape[1]
    row = lambda i: (i, 0)
    in_specs = [pl.BlockSpec((ROW_TILE, k), row),
                pl.BlockSpec((k, d), lambda i: (0, 0)),
                pl.BlockSpec((ROW_TILE, d), row),
                pl.BlockSpec((1, d), lambda i: (0, 0))]
    if final:
        body = _mm_resid_final_kernel
        out_specs = pl.BlockSpec((ROW_TILE, d), row)
        out_shape = jax.ShapeDtypeStruct((m, d), F32)
    else:
        body = _mm_resid_norm_kernel
        out_specs = [pl.BlockSpec((ROW_TILE, d), row), pl.BlockSpec((ROW_TILE, d), row)]
        out_shape = [jax.ShapeDtypeStruct((m, d), F32), jax.ShapeDtypeStruct((m, d), BF16)]
    return pl.pallas_call(
        body,
        grid=(m // ROW_TILE,),
        in_specs=in_specs,
        out_specs=out_specs,
        out_shape=out_shape,
        compiler_params=_params(("parallel",), 48),
        name="matmul_residual_norm",
    )(a, w, hs, g.reshape(1, d))


def _ffn_up_kernel(x_ref, w_ref, wc_ref, bc_ref, o_ref, carry_ref, *, tiles_per_seq):
    i = pl.program_id(0)

    @pl.when(i % tiles_per_seq == 0)
    def _():
        carry_ref[...] = jnp.zeros(carry_ref.shape, F32)

    row = lax.broadcasted_iota(jnp.int32, (SUBLANES, FF_CHUNK), 0)

    def shifted(h, prev, shift):
        rolled = pltpu.roll(h, shift, 0)
        head = rolled[:SUBLANES]
        for i in range(shift):
            head = jnp.where(row == i, prev[SUBLANES - shift + i:SUBLANES - shift + i + 1], head)
        return jnp.concatenate([head, rolled[SUBLANES:]], axis=0)

    def conv(h, cols, c, half):
        prev = carry_ref[half, c]
        out = (bc_ref[:, cols] + wc_ref[0:1, cols] * shifted(h, prev, 2)
               + wc_ref[1:2, cols] * shifted(h, prev, 1) + wc_ref[2:3, cols] * h)
        carry_ref[half, c] = h[FF_ROWS - SUBLANES:FF_ROWS]
        return out

    for r in range(x_ref.shape[0] // FF_ROWS):
        rows = slice(r * FF_ROWS, (r + 1) * FF_ROWS)
        x = x_ref[rows, :]
        for c in range(N_FF_CHUNKS):
            ucols = slice(c * FF_CHUNK, (c + 1) * FF_CHUNK)
            gcols = slice(D_FF + c * FF_CHUNK, D_FF + (c + 1) * FF_CHUNK)
            u = conv(_dot(x, w_ref[:, ucols]), ucols, c, 0)
            g = conv(_dot(x, w_ref[:, gcols]), gcols, c, 1)
            act = g * u / (1.0 + jnp.exp(-g))
            o_ref[rows, ucols] = act.astype(o_ref.dtype)


def _ffn_up(xn, w, w_conv, b_conv, seq_len):
    m, k = xn.shape
    kern = functools.partial(_ffn_up_kernel, tiles_per_seq=seq_len // ROW_TILE)
    return pl.pallas_call(
        kern,
        grid=(m // ROW_TILE,),
        in_specs=[pl.BlockSpec((ROW_TILE, k), lambda i: (i, 0)),
                  pl.BlockSpec((k, 2 * D_FF), lambda i: (0, 0)),
                  pl.BlockSpec((CONV_WIDTH, 2 * D_FF), lambda i: (0, 0)),
                  pl.BlockSpec((1, 2 * D_FF), lambda i: (0, 0))],
        out_specs=pl.BlockSpec((ROW_TILE, D_FF), lambda i: (i, 0)),
        out_shape=jax.ShapeDtypeStruct((m, D_FF), BF16),
        scratch_shapes=[pltpu.VMEM((2, N_FF_CHUNKS, SUBLANES, FF_CHUNK), F32)],
        compiler_params=_params(("arbitrary",), 56),
        name="ffn_up_conv_gate",
    )(xn, w, w_conv, b_conv)


def kernel(x, attn_norm, ffn_norm, final_norm, fox_w_qkvf, fox_b_f, fox_w_o,
           sb_w_qkv, sb_w_o, ffn_w_up, ffn_w_conv, ffn_b_conv, ffn_w_down):
    b, s, d = x.shape
    m = b * s
    assert d == D_MODEL and s % TQ == 0 and s % ROW_TILE == 0 and s % VT_GROUP == 0 and TQ == TK

    colscale = jnp.concatenate([jnp.full((1, d), ATTN_SCALE * LOG2E, F32),
                                jnp.ones((1, 2 * d), F32)], axis=1)

    hs = x.reshape(m, d)
    xn = _rmsnorm(hs, attn_norm[0], BF16)
    out = None
    for i in range(DEPTH):
        j = i // 2
        if i % 2 == 0:
            w = fox_w_qkvf[j]
            w_gate = jnp.pad(w[:, 3 * d:], ((0, 0), (0, LANES - N_HEADS))).astype(BF16)
            qkv, f_logit = _proj(xn, w[:, :3 * d].astype(BF16), colscale, w_gate)
            b_gate = jnp.pad(fox_b_f[j], (0, LANES - N_HEADS)).reshape(1, LANES)
            gate = _gate_operand(f_logit.reshape(b, s, LANES), b_gate)
            o = _fox_attention(qkv.reshape(b, s, 3 * d), gate)
            w_o = fox_w_o[j]
        else:
            qkv = _proj(xn, sb_w_qkv[j].astype(BF16), colscale)[0]
            o = _sb_attention(qkv.reshape(b, s, 3 * d))
            w_o = sb_w_o[j]
        hs, xn = _mm_resid_norm(o.reshape(m, d), w_o.astype(BF16), hs, ffn_norm[i])
        act = _ffn_up(xn, ffn_w_up[i].astype(BF16), ffn_w_conv[i], ffn_b_conv[i].reshape(1, -1), s)
        w_down = ffn_w_down[i].astype(BF16)
        if i + 1 < DEPTH:
            hs, xn = _mm_resid_norm(act, w_down, hs, attn_norm[i + 1])
        else:
            out = _mm_resid_norm(act, w_down, hs, final_norm, final=True)
    return out.reshape(b, s, d)
```

```python
import functools
import math

import jax
import jax.numpy as jnp
from jax import lax
from jax.experimental import pallas as pl
from jax.experimental.pallas import tpu as pltpu

D_MODEL = 1024
N_HEADS = 16
HEAD_DIM = D_MODEL // N_HEADS
D_FF = 2816
CONV_WIDTH = 3
DEPTH = 4
NORM_EPS = 1e-6
ATTN_SCALE = HEAD_DIM ** -0.5
LOG2E = math.log2(math.e)

F32 = jnp.float32
BF16 = jnp.bfloat16

LANES = 128
SUBLANES = 8
HEADS_PER_PAIR = LANES // HEAD_DIM
N_HEAD_PAIRS = N_HEADS // HEADS_PER_PAIR
ROW_TILE = 512
RESID_ROW_TILE = 1024
PROJ_CHUNK = 512
FF_CHUNK = 256
N_FF_CHUNKS = D_FF // FF_CHUNK
FF_ROWS = 128
TQ = 512
TK = 512
VT_BLOCK = 256
VT_GROUP = 1024
FOX_UNROLL = 8
SB_UNROLL = 4
CUM_BLOCK = 128
SB_BLOCK = 256
GATE_PARTS = 3
ONES_LANE = GATE_PARTS * N_HEADS
MIB = 2 ** 20


def _params(semantics, vmem_mib):
    return pltpu.CompilerParams(dimension_semantics=semantics,
                                vmem_limit_bytes=vmem_mib * MIB)


def _dot(a, b):
    return jnp.dot(a, b, preferred_element_type=F32)


def _dot_nt(a, b):
    return lax.dot_general(a, b, (((1,), (1,)), ((), ())), preferred_element_type=F32)


def _split_bf16(x, parts):
    pieces = []
    for _ in range(parts - 1):
        p = x.astype(BF16)
        pieces.append(p)
        x = x - p.astype(F32)
    pieces.append(x.astype(BF16))
    return pieces


def _rms(h, g):
    ms = jnp.mean(h * h, axis=-1, keepdims=True)
    return h * lax.rsqrt(ms + NORM_EPS) * g


def _rmsnorm_kernel(x_ref, g_ref, o_ref):
    o_ref[...] = _rms(x_ref[...], g_ref[...]).astype(o_ref.dtype)


def _rmsnorm(x, g, out_dtype):
    m, d = x.shape
    return pl.pallas_call(
        _rmsnorm_kernel,
        grid=(m // ROW_TILE,),
        in_specs=[pl.BlockSpec((ROW_TILE, d), lambda i: (i, 0)),
                  pl.BlockSpec((1, d), lambda i: (0, 0))],
        out_specs=pl.BlockSpec((ROW_TILE, d), lambda i: (i, 0)),
        out_shape=jax.ShapeDtypeStruct((m, d), out_dtype),
        compiler_params=_params(("parallel",), 32),
        name="rmsnorm",
    )(x, g.reshape(1, d))


def _proj_kernel(x_ref, w_ref, cs_ref, o_ref):
    x = x_ref[...]
    n = w_ref.shape[1]
    for c in range(n // PROJ_CHUNK):
        sl = slice(c * PROJ_CHUNK, (c + 1) * PROJ_CHUNK)
        o_ref[:, sl] = (_dot(x, w_ref[:, sl]) * cs_ref[:, sl]).astype(o_ref.dtype)


def _proj_gate_kernel(x_ref, w_ref, cs_ref, wf_ref, o_ref, f_ref):
    _proj_kernel(x_ref, w_ref, cs_ref, o_ref)
    f_ref[...] = _dot(x_ref[...], wf_ref[...])


def _proj(xn, w, colscale, w_gate=None):
    m, k = xn.shape
    n = w.shape[1]
    in_specs = [pl.BlockSpec((ROW_TILE, k), lambda i: (i, 0)),
                pl.BlockSpec((k, n), lambda i: (0, 0)),
                pl.BlockSpec((1, n), lambda i: (0, 0))]
    out_specs = [pl.BlockSpec((ROW_TILE, n), lambda i: (i, 0))]
    out_shape = [jax.ShapeDtypeStruct((m, n), BF16)]
    args = [xn, w, colscale]
    body = _proj_kernel
    if w_gate is not None:
        in_specs.append(pl.BlockSpec((k, LANES), lambda i: (0, 0)))
        out_specs.append(pl.BlockSpec((ROW_TILE, LANES), lambda i: (i, 0)))
        out_shape.append(jax.ShapeDtypeStruct((m, LANES), F32))
        args.append(w_gate)
        body = _proj_gate_kernel
    return pl.pallas_call(
        body,
        grid=(m // ROW_TILE,),
        in_specs=in_specs,
        out_specs=out_specs,
        out_shape=out_shape,
        compiler_params=_params(("parallel",), 48),
        name="qkv_proj",
    )(*args)


def _softplus2(y):
    neg_abs = pltpu.bitcast(pltpu.bitcast(y, jnp.uint32) | jnp.uint32(0x80000000), F32)
    return jnp.maximum(y, 0.0) + jnp.log(1.0 + jnp.exp2(neg_abs)) * LOG2E


def _gate_cumsum_kernel(f_ref, b_ref, tri_ref, place_ref, o_ref):
    n_blocks = f_ref.shape[1] // CUM_BLOCK
    tri = tri_ref[...]
    lane = lax.broadcasted_iota(jnp.int32, (CUM_BLOCK, LANES), 1)
    ones = jnp.where((lane >= ONES_LANE) & (lane < ONES_LANE + GATE_PARTS), 1.0, 0.0)

    def body(r, carry):
        rows = pl.ds(pl.multiple_of(r * CUM_BLOCK, CUM_BLOCK), CUM_BLOCK)
        logit2 = (f_ref[0, rows, :] + b_ref[...]) * LOG2E
        ls = -_softplus2(-logit2)
        c2 = carry
        for piece in _split_bf16(ls, GATE_PARTS):
            c2 = c2 + _dot(tri, piece)
        operand = ones
        for i, piece in enumerate(_split_bf16(c2, GATE_PARTS)):
            operand = operand + _dot(piece, place_ref[i])
        o_ref[0, rows, :] = operand.astype(o_ref.dtype)
        return c2[CUM_BLOCK - 1:CUM_BLOCK, :]

    lax.fori_loop(0, n_blocks, body, jnp.zeros((1, LANES), F32))


def _gate_operand(f_logit, b_gate):
    b, s, _ = f_logit.shape
    idx = jnp.arange(CUM_BLOCK)
    tri = (idx[None, :] <= idx[:, None]).astype(BF16)
    lane = jnp.arange(LANES)
    place = jnp.stack([((lane[:, None] < N_HEADS) & (lane[None, :] == i * N_HEADS + lane[:, None]))
                       for i in range(GATE_PARTS)]).astype(BF16)
    return pl.pallas_call(
        _gate_cumsum_kernel,
        grid=(b,),
        in_specs=[pl.BlockSpec((1, s, LANES), lambda i: (i, 0, 0)),
                  pl.BlockSpec((1, LANES), lambda i: (0, 0)),
                  pl.BlockSpec((CUM_BLOCK, CUM_BLOCK), lambda i: (0, 0)),
                  pl.BlockSpec((GATE_PARTS, LANES, LANES), lambda i: (0, 0, 0))],
        out_specs=pl.BlockSpec((1, s, LANES), lambda i: (i, 0, 0)),
        out_shape=jax.ShapeDtypeStruct((b, s, LANES), BF16),
        compiler_params=_params(("parallel",), 32),
        name="gate_cumsum",
    )(f_logit, b_gate, tri, place)


def _causal(strict):
    shape = (TK, HEADS_PER_PAIR * TQ)
    key = lax.broadcasted_iota(jnp.int32, shape, 0)
    query = lax.broadcasted_iota(jnp.int32, shape, 1) & (TQ - 1)
    return key < query if strict else key <= query


def _rows(index, size):
    if isinstance(index, int):
        return pl.ds(index * size, size)
    return pl.ds(pl.multiple_of(index * size, size), size)


def _transpose_values(v_ref, store):
    row = lax.broadcasted_iota(jnp.int32, (LANES, LANES), 0)
    col = lax.broadcasted_iota(jnp.int32, (LANES, LANES), 1)
    eye = (row == col).astype(BF16)

    group = VT_GROUP // VT_BLOCK

    def body(i, carry):
        vt = _dot_nt(eye, v_ref[0, _rows(i, VT_GROUP), :]).astype(BF16)
        for g in range(group):
            store(i * group + g, vt[:, g * VT_BLOCK:(g + 1) * VT_BLOCK])
        return carry

    lax.fori_loop(0, v_ref.shape[1] // VT_GROUP, body, 0)


def _value_blocks(vt_ref, j, *lead):
    per_block = TK // VT_BLOCK
    return jnp.concatenate([vt_ref[(*lead, j * per_block + i)] for i in range(per_block)], axis=1)


def _pipelined_sweep(n_tiles, stages, descending, unroll):
    depth = len(stages)
    pairs = [(t, j) for t in range(n_tiles)
             for j in (range(t, -1, -1) if descending else range(t + 1))]
    n = len(pairs)
    assert n >= depth - 1 + unroll and unroll % 2 == 0

    def run(step_pairs, parity, masked, active):
        for s in active:
            slot = parity if s % 2 == 0 else 1 - parity
            tile, block = step_pairs[s]
            if s == 0:
                stages[0](tile, block, slot, masked)
            else:
                stages[s](tile, block, slot)

    def static_step(u):
        active = [s for s in range(depth) if 0 <= u - s < n]
        step_pairs = {s: pairs[u - s] for s in active}
        masked = 0 in step_pairs and step_pairs[0][0] == step_pairs[0][1]
        run(step_pairs, u & 1, masked, active)

    def advance(tile, block):
        if descending:
            last = block == 0
            return jnp.where(last, tile + 1, tile), jnp.where(last, tile + 1, block - 1)
        last = block == tile
        return jnp.where(last, tile + 1, tile), jnp.where(last, 0, block + 1)

    start = depth - 1 + (n - depth + 1) % unroll
    for u in range(start):
        static_step(u)

    def on_diagonal(pair):
        return pair[0] == pair[1]

    mask_cases = sorted({tuple(on_diagonal(pairs[u + i]) for i in range(unroll))
                         for u in range(start, n, unroll)})

    def body(_, carry):
        window = [(carry[2 * s], carry[2 * s + 1]) for s in range(depth)]
        for _ in range(unroll - 1):
            window.insert(0, advance(*window[0]))
        heads = window[:unroll][::-1]
        for masked in mask_cases:
            cond = on_diagonal(heads[0]) == masked[0]
            for i in range(1, unroll):
                cond = jnp.logical_and(cond, on_diagonal(heads[i]) == masked[i])

            @pl.when(cond)
            def _():
                for i in range(unroll):
                    offset = unroll - 1 - i
                    step_pairs = {s: window[offset + s] for s in range(depth)}
                    run(step_pairs, (start + i) & 1, masked[i], range(depth))
        window.insert(0, advance(*window[0]))
        return tuple(v for pair in window[:depth] for v in pair)

    init = ()
    for s in range(depth):
        init += tuple(jnp.int32(v) for v in pairs[start - s])
    lax.fori_loop(0, (n - start) // unroll, body, init)

    for u in range(n, n + depth - 1):
        static_step(u)


def _store_outputs(o_ref, n_tiles, normalized):
    def body(t, carry):
        acc_t = normalized(t)
        out_t = jnp.concatenate([acc_t[:HEAD_DIM, :TQ], acc_t[HEAD_DIM:, TQ:]], axis=0)
        o_ref[0, _rows(t, TQ), :] = out_t.T.astype(o_ref.dtype)
        return carry

    lax.fori_loop(0, n_tiles, body, 0)


def _fox_kernel(q_ref, k_ref, v_ref, gate_ref, perm_ref, o_ref,
                vt_ref, qa_ref, m_ref, acc_ref, s_buf, m_buf, a_buf):
    pair = pl.program_id(1)
    n_tiles = q_ref.shape[1] // TQ

    row = lax.broadcasted_iota(jnp.int32, (LANES, VT_BLOCK), 0)

    def store(j, vt):
        one = jnp.ones_like(vt)
        vt_ref[0, j] = jnp.where(row < HEAD_DIM, vt, one)
        vt_ref[1, j] = jnp.where(row >= HEAD_DIM, vt, one)

    _transpose_values(v_ref, store)

    q = q_ref[0]
    gate_all = gate_ref[0]
    zero = jnp.zeros_like(q)
    lane = lax.broadcasted_iota(jnp.int32, q.shape, 1)
    lane_row = lax.broadcasted_iota(jnp.int32, (1, LANES), 1)
    for hh in range(HEADS_PER_PAIR):
        h = HEADS_PER_PAIR * pair + hh
        minus = jnp.zeros((1, LANES), F32)
        for i in range(GATE_PARTS):
            minus = jnp.where(lane_row == i * N_HEADS + h, -1.0, minus)
        gate_lanes = (_dot(gate_all, perm_ref[h]) + minus).astype(BF16)
        feat = (lane < HEAD_DIM) if hh == 0 else (lane >= HEAD_DIM)
        operand = jnp.concatenate([jnp.where(feat, q, zero), gate_lanes], axis=1)
        qa_ref[:, hh * TQ:(hh + 1) * TQ, :] = operand.reshape(n_tiles, TQ, 2 * LANES)
    m_ref[...] = jnp.full(m_ref.shape, -jnp.inf, F32)
    acc_ref[...] = jnp.zeros(acc_ref.shape, F32)

    def scores(t, j, slot, masked):
        rows = _rows(j, TK)
        k_aug = jnp.concatenate([k_ref[0, rows, :], gate_ref[0, rows, :]], axis=1)
        s = _dot_nt(k_aug, qa_ref[t])
        if masked:
            s = jnp.where(_causal(strict=False), s, -jnp.inf)
        m_old = m_ref[t]
        m_new = jnp.maximum(m_old, jnp.max(s, axis=0, keepdims=True))
        m_ref[t] = m_new
        m_buf[slot] = m_new
        a_buf[slot] = jnp.exp2(m_old - m_new)
        s_buf[slot] = s

    def accumulate(t, j, slot):
        prob = jnp.exp2(s_buf[slot] - m_buf[slot]).astype(BF16)
        pv = jnp.concatenate(
            [_dot(_value_blocks(vt_ref, j, hh), prob[:, hh * TQ:(hh + 1) * TQ])
             for hh in range(HEADS_PER_PAIR)], axis=1)
        acc_ref[t] = acc_ref[t] * a_buf[slot] + pv

    _pipelined_sweep(n_tiles, (scores, accumulate), descending=False, unroll=FOX_UNROLL)

    def normalized(t):
        acc = acc_ref[t]
        denom = jnp.concatenate(
            [jnp.broadcast_to(acc[HEAD_DIM:HEAD_DIM + 1, :TQ], (LANES, TQ)),
             jnp.broadcast_to(acc[0:1, TQ:], (LANES, TQ))], axis=1)
        return acc / denom

    _store_outputs(o_ref, n_tiles, normalized)


def _sb_kernel(q_ref, k_ref, v_ref, tt_ref, o_ref,
               vt_ref, qs_ref, r_ref, acc_ref, z_buf, w_buf):
    n_tiles = q_ref.shape[1] // TQ

    def store(j, vt):
        vt_ref[j] = vt

    _transpose_values(v_ref, store)

    q = q_ref[0]
    zero = jnp.zeros_like(q)
    lane = lax.broadcasted_iota(jnp.int32, q.shape, 1)
    for hh in range(HEADS_PER_PAIR):
        feat = (lane < HEAD_DIM) if hh == 0 else (lane >= HEAD_DIM)
        qs_ref[:, hh * TQ:(hh + 1) * TQ, :] = jnp.where(feat, q, zero).reshape(n_tiles, TQ, LANES)
    r_ref[...] = jnp.zeros(r_ref.shape, F32)
    acc_ref[...] = jnp.zeros(acc_ref.shape, F32)

    n_sub = TK // SB_BLOCK

    def logits(t, j, slot, masked):
        z = _dot_nt(k_ref[0, _rows(j, TK), :], qs_ref[t])
        if masked:
            z = jnp.where(_causal(strict=True), z, -jnp.inf)
        z_buf[slot] = z

    def weights(t, j, slot):
        r = r_ref[t]
        for g in reversed(range(n_sub)):
            rows = slice(g * SB_BLOCK, (g + 1) * SB_BLOCK)
            z = z_buf[slot, rows, :]
            sp = _softplus2(z)
            sums = _dot(tt_ref[...], sp.astype(BF16))
            w_buf[slot, rows, :] = jnp.exp2(z - sp - sums[:SB_BLOCK] - r).astype(BF16)
            r = r + sums[SB_BLOCK:SB_BLOCK + 1]
        r_ref[t] = r

    def accumulate(t, j, slot):
        acc_ref[t] += _dot(_value_blocks(vt_ref, j), w_buf[slot])

    _pipelined_sweep(n_tiles, (logits, weights, accumulate), descending=True, unroll=SB_UNROLL)
    _store_outputs(o_ref, n_tiles, lambda t: acc_ref[t])


def _attention_call(body, name, qkv, extra_inputs, extra_specs, scratch_shapes):
    b, s, _ = qkv.shape

    def column_block(offset):
        return pl.BlockSpec((1, s, LANES), lambda b, p: (b, 0, offset + p))

    return pl.pallas_call(
        body,
        grid=(b, N_HEAD_PAIRS),
        in_specs=[column_block(0), column_block(N_HEAD_PAIRS), column_block(2 * N_HEAD_PAIRS)
                  ] + extra_specs,
        out_specs=column_block(0),
        out_shape=jax.ShapeDtypeStruct((b, s, D_MODEL), BF16),
        scratch_shapes=scratch_shapes,
        compiler_params=_params(("parallel", "parallel"), 56),
        name=name,
    )(qkv, qkv, qkv, *extra_inputs)


def _fox_attention(qkv, gate):
    b, s, _ = qkv.shape
    n_tiles = s // TQ
    lane = jnp.arange(LANES)
    perm = jnp.stack([
        sum(((lane[:, None] == i * N_HEADS + h) & (lane[None, :] == ONES_LANE + i))
            for i in range(GATE_PARTS))
        for h in range(N_HEADS)]).astype(BF16)
    cols = HEADS_PER_PAIR * TQ
    return _attention_call(
        _fox_kernel, "fox_attention", qkv, [gate, perm],
        [pl.BlockSpec((1, s, LANES), lambda b, p: (b, 0, 0)),
         pl.BlockSpec((N_HEADS, LANES, LANES), lambda b, p: (0, 0, 0))],
        [pltpu.VMEM((HEADS_PER_PAIR, s // VT_BLOCK, LANES, VT_BLOCK), BF16),
         pltpu.VMEM((n_tiles, cols, 2 * LANES), BF16),
         pltpu.VMEM((n_tiles, 1, cols), F32),
         pltpu.VMEM((n_tiles, LANES, cols), F32),
         pltpu.VMEM((2, TK, cols), F32),
         pltpu.VMEM((2, 1, cols), F32),
         pltpu.VMEM((2, 1, cols), F32)])


def _sb_attention(qkv):
    b, s, _ = qkv.shape
    n_tiles = s // TQ
    idx = jnp.arange(SB_BLOCK)
    later = (idx[None, :] > idx[:, None]).astype(BF16)
    tt = jnp.concatenate([later, jnp.ones((SUBLANES, SB_BLOCK), BF16)], axis=0)
    cols = HEADS_PER_PAIR * TQ
    return _attention_call(
        _sb_kernel, "sb_attention", qkv, [tt],
        [pl.BlockSpec((SB_BLOCK + SUBLANES, SB_BLOCK), lambda b, p: (0, 0))],
        [pltpu.VMEM((s // VT_BLOCK, LANES, VT_BLOCK), BF16),
         pltpu.VMEM((n_tiles, cols, LANES), BF16),
         pltpu.VMEM((n_tiles, 1, cols), F32),
         pltpu.VMEM((n_tiles, LANES, cols), F32),
         pltpu.VMEM((2, TK, cols), F32),
         pltpu.VMEM((2, TK, cols), BF16)])


def _mm_resid_norm_kernel(a_ref, w_ref, h_ref, g_ref, hs_ref, xn_ref):
    h = _dot(a_ref[...], w_ref[...]) + h_ref[...]
    hs_ref[...] = h
    xn_ref[...] = _rms(h, g_ref[...]).astype(xn_ref.dtype)


def _mm_resid_final_kernel(a_ref, w_ref, h_ref, g_ref, o_ref):
    h = _dot(a_ref[...], w_ref[...]) + h_ref[...]
    o_ref[...] = _rms(h, g_ref[...]).astype(o_ref.dtype)


def _mm_resid_norm(a, w, hs, g, final=False):
    m, k = a.shape
    d = w.shape[1]
    tile = RESID_ROW_TILE
    row = lambda i: (i, 0)
    in_specs = [pl.BlockSpec((tile, k), row),
                pl.BlockSpec((k, d), lambda i: (0, 0)),
                pl.BlockSpec((tile, d), row),
                pl.BlockSpec((1, d), lambda i: (0, 0))]
    if final:
        body = _mm_resid_final_kernel
        out_specs = pl.BlockSpec((tile, d), row)
        out_shape = jax.ShapeDtypeStruct((m, d), F32)
    else:
        body = _mm_resid_norm_kernel
        out_specs = [pl.BlockSpec((tile, d), row), pl.BlockSpec((tile, d), row)]
        out_shape = [jax.ShapeDtypeStruct((m, d), F32), jax.ShapeDtypeStruct((m, d), BF16)]
    return pl.pallas_call(
        body,
        grid=(m // tile,),
        in_specs=in_specs,
        out_specs=out_specs,
        out_shape=out_shape,
        compiler_params=_params(("parallel",), 56),
        name="matmul_residual_norm",
    )(a, w, hs, g.reshape(1, d))


def _ffn_up_kernel(x_ref, w_ref, wc_ref, bc_ref, o_ref, carry_ref, *, tiles_per_seq):
    i = pl.program_id(0)

    @pl.when(i % tiles_per_seq == 0)
    def _():
        carry_ref[...] = jnp.zeros(carry_ref.shape, F32)

    row = lax.broadcasted_iota(jnp.int32, (SUBLANES, FF_CHUNK), 0)

    def shifted(h, prev, shift):
        rolled = pltpu.roll(h, shift, 0)
        head = rolled[:SUBLANES]
        for i in range(shift):
            head = jnp.where(row == i, prev[SUBLANES - shift + i:SUBLANES - shift + i + 1], head)
        return jnp.concatenate([head, rolled[SUBLANES:]], axis=0)

    def conv(h, cols, c, half):
        prev = carry_ref[half, c]
        out = (bc_ref[:, cols] + wc_ref[0:1, cols] * shifted(h, prev, 2)
               + wc_ref[1:2, cols] * shifted(h, prev, 1) + wc_ref[2:3, cols] * h)
        carry_ref[half, c] = h[FF_ROWS - SUBLANES:FF_ROWS]
        return out

    for r in range(x_ref.shape[0] // FF_ROWS):
        rows = slice(r * FF_ROWS, (r + 1) * FF_ROWS)
        x = x_ref[rows, :]
        for c in range(N_FF_CHUNKS):
            ucols = slice(c * FF_CHUNK, (c + 1) * FF_CHUNK)
            gcols = slice(D_FF + c * FF_CHUNK, D_FF + (c + 1) * FF_CHUNK)
            u = conv(_dot(x, w_ref[:, ucols]), ucols, c, 0)
            g = conv(_dot(x, w_ref[:, gcols]), gcols, c, 1)
            act = g * u / (1.0 + jnp.exp(-g))
            o_ref[rows, ucols] = act.astype(o_ref.dtype)


def _ffn_up(xn, w, w_conv, b_conv, seq_len):
    m, k = xn.shape
    kern = functools.partial(_ffn_up_kernel, tiles_per_seq=seq_len // ROW_TILE)
    return pl.pallas_call(
        kern,
        grid=(m // ROW_TILE,),
        in_specs=[pl.BlockSpec((ROW_TILE, k), lambda i: (i, 0)),
                  pl.BlockSpec((k, 2 * D_FF), lambda i: (0, 0)),
                  pl.BlockSpec((CONV_WIDTH, 2 * D_FF), lambda i: (0, 0)),
                  pl.BlockSpec((1, 2 * D_FF), lambda i: (0, 0))],
        out_specs=pl.BlockSpec((ROW_TILE, D_FF), lambda i: (i, 0)),
        out_shape=jax.ShapeDtypeStruct((m, D_FF), BF16),
        scratch_shapes=[pltpu.VMEM((2, N_FF_CHUNKS, SUBLANES, FF_CHUNK), F32)],
        compiler_params=_params(("arbitrary",), 56),
        name="ffn_up_conv_gate",
    )(xn, w, w_conv, b_conv)


def kernel(x, attn_norm, ffn_norm, final_norm, fox_w_qkvf, fox_b_f, fox_w_o,
           sb_w_qkv, sb_w_o, ffn_w_up, ffn_w_conv, ffn_b_conv, ffn_w_down):
    b, s, d = x.shape
    m = b * s
    assert d == D_MODEL and s % TQ == 0 and s % ROW_TILE == 0 and s % VT_GROUP == 0 and TQ == TK

    colscale = jnp.concatenate([jnp.full((1, d), ATTN_SCALE * LOG2E, F32),
                                jnp.ones((1, 2 * d), F32)], axis=1)

    hs = x.reshape(m, d)
    xn = _rmsnorm(hs, attn_norm[0], BF16)
    out = None
    for i in range(DEPTH):
        j = i // 2
        if i % 2 == 0:
            w = fox_w_qkvf[j]
            w_gate = jnp.pad(w[:, 3 * d:], ((0, 0), (0, LANES - N_HEADS))).astype(BF16)
            qkv, f_logit = _proj(xn, w[:, :3 * d].astype(BF16), colscale, w_gate)
            b_gate = jnp.pad(fox_b_f[j], (0, LANES - N_HEADS)).reshape(1, LANES)
            gate = _gate_operand(f_logit.reshape(b, s, LANES), b_gate)
            o = _fox_attention(qkv.reshape(b, s, 3 * d), gate)
            w_o = fox_w_o[j]
        else:
            qkv = _proj(xn, sb_w_qkv[j].astype(BF16), colscale)[0]
            o = _sb_attention(qkv.reshape(b, s, 3 * d))
            w_o = sb_w_o[j]
        hs, xn = _mm_resid_norm(o.reshape(m, d), w_o.astype(BF16), hs, ffn_norm[i])
        act = _ffn_up(xn, ffn_w_up[i].astype(BF16), ffn_w_conv[i], ffn_b_conv[i].reshape(1, -1), s)
        w_down = ffn_w_down[i].astype(BF16)
        if i + 1 < DEPTH:
            hs, xn = _mm_resid_norm(act, w_down, hs, attn_norm[i + 1])
        else:
            out = _mm_resid_norm(act, w_down, hs, final_norm, final=True)
    return out.reshape(b, s, d)
```

```python
import functools
import math

import jax
import jax.numpy as jnp
from jax import lax
from jax.experimental import pallas as pl
from jax.experimental.pallas import tpu as pltpu

D_MODEL = 1024
N_HEADS = 16
HEAD_DIM = D_MODEL // N_HEADS
D_FF = 2816
CONV_WIDTH = 3
DEPTH = 4
NORM_EPS = 1e-6
ATTN_SCALE = HEAD_DIM ** -0.5
LOG2E = math.log2(math.e)

F32 = jnp.float32
BF16 = jnp.bfloat16

LANES = 128
SUBLANES = 8
HEADS_PER_PAIR = LANES // HEAD_DIM
N_HEAD_PAIRS = N_HEADS // HEADS_PER_PAIR
ROW_TILE = 512
RESID_ROW_TILE = 1024
PROJ_CHUNK = 512
FF_CHUNK = 256
N_FF_CHUNKS = D_FF // FF_CHUNK
FF_ROWS = 128
TQ = 512
TK = 512
VT_BLOCK = 256
VT_GROUP = 1024
FOX_UNROLL = 8
SB_UNROLL = 4
CUM_BLOCK = 128
SB_BLOCK = 256
GATE_PARTS = 3
ONES_LANE = GATE_PARTS * N_HEADS
MIB = 2 ** 20


def _params(semantics, vmem_mib):
    return pltpu.CompilerParams(dimension_semantics=semantics,
                                vmem_limit_bytes=vmem_mib * MIB)


def _dot(a, b):
    return jnp.dot(a, b, preferred_element_type=F32)


def _dot_nt(a, b):
    return lax.dot_general(a, b, (((1,), (1,)), ((), ())), preferred_element_type=F32)


def _split_bf16(x, parts):
    pieces = []
    for _ in range(parts - 1):
        p = x.astype(BF16)
        pieces.append(p)
        x = x - p.astype(F32)
    pieces.append(x.astype(BF16))
    return pieces


def _rms(h, g):
    ms = jnp.mean(h * h, axis=-1, keepdims=True)
    return h * lax.rsqrt(ms + NORM_EPS) * g


def _rmsnorm_kernel(x_ref, g_ref, o_ref):
    o_ref[...] = _rms(x_ref[...], g_ref[...]).astype(o_ref.dtype)


def _rmsnorm(x, g, out_dtype):
    m, d = x.shape
    return pl.pallas_call(
        _rmsnorm_kernel,
        grid=(m // ROW_TILE,),
        in_specs=[pl.BlockSpec((ROW_TILE, d), lambda i: (i, 0)),
                  pl.BlockSpec((1, d), lambda i: (0, 0))],
        out_specs=pl.BlockSpec((ROW_TILE, d), lambda i: (i, 0)),
        out_shape=jax.ShapeDtypeStruct((m, d), out_dtype),
        compiler_params=_params(("parallel",), 32),
        name="rmsnorm",
    )(x, g.reshape(1, d))


def _proj_kernel(x_ref, w_ref, cs_ref, o_ref):
    x = x_ref[...]
    n = w_ref.shape[1]
    for c in range(n // PROJ_CHUNK):
        sl = slice(c * PROJ_CHUNK, (c + 1) * PROJ_CHUNK)
        o_ref[:, sl] = (_dot(x, w_ref[:, sl]) * cs_ref[:, sl]).astype(o_ref.dtype)


def _proj_gate_kernel(x_ref, w_ref, cs_ref, wf_ref, o_ref, f_ref):
    _proj_kernel(x_ref, w_ref, cs_ref, o_ref)
    f_ref[...] = _dot(x_ref[...], wf_ref[...])


def _proj(xn, w, colscale, w_gate=None):
    m, k = xn.shape
    n = w.shape[1]
    in_specs = [pl.BlockSpec((ROW_TILE, k), lambda i: (i, 0)),
                pl.BlockSpec((k, n), lambda i: (0, 0)),
                pl.BlockSpec((1, n), lambda i: (0, 0))]
    out_specs = [pl.BlockSpec((ROW_TILE, n), lambda i: (i, 0))]
    out_shape = [jax.ShapeDtypeStruct((m, n), BF16)]
    args = [xn, w, colscale]
    body = _proj_kernel
    if w_gate is not None:
        in_specs.append(pl.BlockSpec((k, LANES), lambda i: (0, 0)))
        out_specs.append(pl.BlockSpec((ROW_TILE, LANES), lambda i: (i, 0)))
        out_shape.append(jax.ShapeDtypeStruct((m, LANES), F32))
        args.append(w_gate)
        body = _proj_gate_kernel
    return pl.pallas_call(
        body,
        grid=(m // ROW_TILE,),
        in_specs=in_specs,
        out_specs=out_specs,
        out_shape=out_shape,
        compiler_params=_params(("parallel",), 48),
        name="qkv_proj",
    )(*args)


def _softplus2(y):
    neg_abs = pltpu.bitcast(pltpu.bitcast(y, jnp.uint32) | jnp.uint32(0x80000000), F32)
    return jnp.maximum(y, 0.0) + jnp.log(1.0 + jnp.exp2(neg_abs)) * LOG2E


def _gate_cumsum_kernel(f_ref, b_ref, tri_ref, place_ref, o_ref):
    n_blocks = f_ref.shape[1] // CUM_BLOCK
    tri = tri_ref[...]
    lane = lax.broadcasted_iota(jnp.int32, (CUM_BLOCK, LANES), 1)
    ones = jnp.where((lane >= ONES_LANE) & (lane < ONES_LANE + GATE_PARTS), 1.0, 0.0)

    def body(r, carry):
        rows = pl.ds(pl.multiple_of(r * CUM_BLOCK, CUM_BLOCK), CUM_BLOCK)
        logit2 = (f_ref[0, rows, :] + b_ref[...]) * LOG2E
        ls = -_softplus2(-logit2)
        c2 = carry
        for piece in _split_bf16(ls, GATE_PARTS):
            c2 = c2 + _dot(tri, piece)
        operand = ones
        for i, piece in enumerate(_split_bf16(c2, GATE_PARTS)):
            operand = operand + _dot(piece, place_ref[i])
        o_ref[0, rows, :] = operand.astype(o_ref.dtype)
        return c2[CUM_BLOCK - 1:CUM_BLOCK, :]

    lax.fori_loop(0, n_blocks, body, jnp.zeros((1, LANES), F32))


def _gate_operand(f_logit, b_gate):
    b, s, _ = f_logit.shape
    idx = jnp.arange(CUM_BLOCK)
    tri = (idx[None, :] <= idx[:, None]).astype(BF16)
    lane = jnp.arange(LANES)
    place = jnp.stack([((lane[:, None] < N_HEADS) & (lane[None, :] == i * N_HEADS + lane[:, None]))
                       for i in range(GATE_PARTS)]).astype(BF16)
    return pl.pallas_call(
        _gate_cumsum_kernel,
        grid=(b,),
        in_specs=[pl.BlockSpec((1, s, LANES), lambda i: (i, 0, 0)),
                  pl.BlockSpec((1, LANES), lambda i: (0, 0)),
                  pl.BlockSpec((CUM_BLOCK, CUM_BLOCK), lambda i: (0, 0)),
                  pl.BlockSpec((GATE_PARTS, LANES, LANES), lambda i: (0, 0, 0))],
        out_specs=pl.BlockSpec((1, s, LANES), lambda i: (i, 0, 0)),
        out_shape=jax.ShapeDtypeStruct((b, s, LANES), BF16),
        compiler_params=_params(("parallel",), 32),
        name="gate_cumsum",
    )(f_logit, b_gate, tri, place)


def _causal(strict):
    shape = (TK, HEADS_PER_PAIR * TQ)
    key = lax.broadcasted_iota(jnp.int32, shape, 0)
    query = lax.broadcasted_iota(jnp.int32, shape, 1) & (TQ - 1)
    return key < query if strict else key <= query


def _rows(index, size):
    if isinstance(index, int):
        return pl.ds(index * size, size)
    return pl.ds(pl.multiple_of(index * size, size), size)


def _transpose_values(v_ref, store):
    row = lax.broadcasted_iota(jnp.int32, (LANES, LANES), 0)
    col = lax.broadcasted_iota(jnp.int32, (LANES, LANES), 1)
    eye = (row == col).astype(BF16)

    group = VT_GROUP // VT_BLOCK

    def body(i, carry):
        vt = _dot_nt(eye, v_ref[0, _rows(i, VT_GROUP), :]).astype(BF16)
        for g in range(group):
            store(i * group + g, vt[:, g * VT_BLOCK:(g + 1) * VT_BLOCK])
        return carry

    lax.fori_loop(0, v_ref.shape[1] // VT_GROUP, body, 0)


def _value_blocks(vt_ref, j, *lead):
    per_block = TK // VT_BLOCK
    return jnp.concatenate([vt_ref[(*lead, j * per_block + i)] for i in range(per_block)], axis=1)


def _pipelined_sweep(n_tiles, stages, descending, unroll):
    depth = len(stages)
    pairs = [(t, j) for t in range(n_tiles)
             for j in (range(t, -1, -1) if descending else range(t + 1))]
    n = len(pairs)
    assert n >= depth - 1 + unroll and unroll % 2 == 0

    def run(step_pairs, parity, masked, active):
        for s in active:
            slot = parity if s % 2 == 0 else 1 - parity
            tile, block = step_pairs[s]
            if s == 0:
                stages[0](tile, block, slot, masked)
            else:
                stages[s](tile, block, slot)

    def static_step(u):
        active = [s for s in range(depth) if 0 <= u - s < n]
        step_pairs = {s: pairs[u - s] for s in active}
        masked = 0 in step_pairs and step_pairs[0][0] == step_pairs[0][1]
        run(step_pairs, u & 1, masked, active)

    def advance(tile, block):
        if descending:
            last = block == 0
            return jnp.where(last, tile + 1, tile), jnp.where(last, tile + 1, block - 1)
        last = block == tile
        return jnp.where(last, tile + 1, tile), jnp.where(last, 0, block + 1)

    start = depth - 1 + (n - depth + 1) % unroll
    for u in range(start):
        static_step(u)

    def on_diagonal(pair):
        return pair[0] == pair[1]

    mask_cases = sorted({tuple(on_diagonal(pairs[u + i]) for i in range(unroll))
                         for u in range(start, n, unroll)})

    def body(_, carry):
        window = [(carry[2 * s], carry[2 * s + 1]) for s in range(depth)]
        for _ in range(unroll - 1):
            window.insert(0, advance(*window[0]))
        heads = window[:unroll][::-1]
        for masked in mask_cases:
            cond = on_diagonal(heads[0]) == masked[0]
            for i in range(1, unroll):
                cond = jnp.logical_and(cond, on_diagonal(heads[i]) == masked[i])

            @pl.when(cond)
            def _():
                for i in range(unroll):
                    offset = unroll - 1 - i
                    step_pairs = {s: window[offset + s] for s in range(depth)}
                    run(step_pairs, (start + i) & 1, masked[i], range(depth))
        window.insert(0, advance(*window[0]))
        return tuple(v for pair in window[:depth] for v in pair)

    init = ()
    for s in range(depth):
        init += tuple(jnp.int32(v) for v in pairs[start - s])
    lax.fori_loop(0, (n - start) // unroll, body, init)

    for u in range(n, n + depth - 1):
        static_step(u)


def _store_outputs(o_ref, n_tiles, normalized):
    def body(t, carry):
        acc_t = normalized(t)
        out_t = jnp.concatenate([acc_t[:HEAD_DIM, :TQ], acc_t[HEAD_DIM:, TQ:]], axis=0)
        o_ref[0, _rows(t, TQ), :] = out_t.T.astype(o_ref.dtype)
        return carry

    lax.fori_loop(0, n_tiles, body, 0)


def _fox_kernel(q_ref, k_ref, v_ref, gate_ref, perm_ref, o_ref,
                vt_ref, qa_ref, m_ref, acc_ref, s_buf, m_buf, a_buf):
    pair = pl.program_id(1)
    n_tiles = q_ref.shape[1] // TQ

    row = lax.broadcasted_iota(jnp.int32, (LANES, VT_BLOCK), 0)

    def store(j, vt):
        one = jnp.ones_like(vt)
        vt_ref[0, j] = jnp.where(row < HEAD_DIM, vt, one)
        vt_ref[1, j] = jnp.where(row >= HEAD_DIM, vt, one)

    _transpose_values(v_ref, store)

    q = q_ref[0]
    gate_all = gate_ref[0]
    zero = jnp.zeros_like(q)
    lane = lax.broadcasted_iota(jnp.int32, q.shape, 1)
    lane_row = lax.broadcasted_iota(jnp.int32, (1, LANES), 1)
    for hh in range(HEADS_PER_PAIR):
        h = HEADS_PER_PAIR * pair + hh
        minus = jnp.zeros((1, LANES), F32)
        for i in range(GATE_PARTS):
            minus = jnp.where(lane_row == i * N_HEADS + h, -1.0, minus)
        gate_lanes = (_dot(gate_all, perm_ref[h]) + minus).astype(BF16)
        feat = (lane < HEAD_DIM) if hh == 0 else (lane >= HEAD_DIM)
        operand = jnp.concatenate([jnp.where(feat, q, zero), gate_lanes], axis=1)
        qa_ref[:, hh * TQ:(hh + 1) * TQ, :] = operand.reshape(n_tiles, TQ, 2 * LANES)
    m_ref[...] = jnp.full(m_ref.shape, -jnp.inf, F32)
    acc_ref[...] = jnp.zeros(acc_ref.shape, F32)

    def scores(t, j, slot, masked):
        rows = _rows(j, TK)
        k_aug = jnp.concatenate([k_ref[0, rows, :], gate_ref[0, rows, :]], axis=1)
        s = _dot_nt(k_aug, qa_ref[t])
        if masked:
            s = jnp.where(_causal(strict=False), s, -jnp.inf)
        m_old = m_ref[t]
        m_new = jnp.maximum(m_old, jnp.max(s, axis=0, keepdims=True))
        m_ref[t] = m_new
        m_buf[slot] = m_new
        a_buf[slot] = jnp.exp2(m_old - m_new)
        s_buf[slot] = s

    def accumulate(t, j, slot):
        prob = jnp.exp2(s_buf[slot] - m_buf[slot]).astype(BF16)
        pv = jnp.concatenate(
            [_dot(_value_blocks(vt_ref, j, hh), prob[:, hh * TQ:(hh + 1) * TQ])
             for hh in range(HEADS_PER_PAIR)], axis=1)
        acc_ref[t] = acc_ref[t] * a_buf[slot] + pv

    _pipelined_sweep(n_tiles, (scores, accumulate), descending=False, unroll=FOX_UNROLL)

    def normalized(t):
        acc = acc_ref[t]
        denom = jnp.concatenate(
            [jnp.broadcast_to(acc[HEAD_DIM:HEAD_DIM + 1, :TQ], (LANES, TQ)),
             jnp.broadcast_to(acc[0:1, TQ:], (LANES, TQ))], axis=1)
        return acc / denom

    _store_outputs(o_ref, n_tiles, normalized)


def _sb_kernel(q_ref, k_ref, v_ref, tt_ref, o_ref,
               vt_ref, qs_ref, r_ref, acc_ref, z_buf, w_buf):
    n_tiles = q_ref.shape[1] // TQ

    def store(j, vt):
        vt_ref[j] = vt

    _transpose_values(v_ref, store)

    q = q_ref[0]
    zero = jnp.zeros_like(q)
    lane = lax.broadcasted_iota(jnp.int32, q.shape, 1)
    for hh in range(HEADS_PER_PAIR):
        feat = (lane < HEAD_DIM) if hh == 0 else (lane >= HEAD_DIM)
        qs_ref[:, hh * TQ:(hh + 1) * TQ, :] = jnp.where(feat, q, zero).reshape(n_tiles, TQ, LANES)
    r_ref[...] = jnp.zeros(r_ref.shape, F32)
    acc_ref[...] = jnp.zeros(acc_ref.shape, F32)

    n_sub = TK // SB_BLOCK

    def logits(t, j, slot, masked):
        z = _dot_nt(k_ref[0, _rows(j, TK), :], qs_ref[t])
        if masked:
            z = jnp.where(_causal(strict=True), z, -jnp.inf)
        z_buf[slot] = z

    def weights(t, j, slot):
        r = r_ref[t]
        for g in reversed(range(n_sub)):
            rows = slice(g * SB_BLOCK, (g + 1) * SB_BLOCK)
            z = z_buf[slot, rows, :]
            sp = _softplus2(z)
            sums = _dot(tt_ref[...], sp.astype(BF16))
            w_buf[slot, rows, :] = jnp.exp2(z - sp - sums[:SB_BLOCK] - r).astype(BF16)
            r = r + sums[SB_BLOCK:SB_BLOCK + 1]
        r_ref[t] = r

    def accumulate(t, j, slot):
        acc_ref[t] += _dot(_value_blocks(vt_ref, j), w_buf[slot])

    _pipelined_sweep(n_tiles, (logits, weights, accumulate), descending=True, unroll=SB_UNROLL)
    _store_outputs(o_ref, n_tiles, lambda t: acc_ref[t])


def _attention_call(body, name, qkv, extra_inputs, extra_specs, scratch_shapes):
    b, s, _ = qkv.shape

    def column_block(offset):
        return pl.BlockSpec((1, s, LANES), lambda b, p: (b, 0, offset + p))

    return pl.pallas_call(
        body,
        grid=(b, N_HEAD_PAIRS),
        in_specs=[column_block(0), column_block(N_HEAD_PAIRS), column_block(2 * N_HEAD_PAIRS)
                  ] + extra_specs,
        out_specs=column_block(0),
        out_shape=jax.ShapeDtypeStruct((b, s, D_MODEL), BF16),
        scratch_shapes=scratch_shapes,
        compiler_params=_params(("parallel", "parallel"), 56),
        name=name,
    )(qkv, qkv, qkv, *extra_inputs)


def _fox_attention(qkv, gate):
    b, s, _ = qkv.shape
    n_tiles = s // TQ
    lane = jnp.arange(LANES)
    perm = jnp.stack([
        sum(((lane[:, None] == i * N_HEADS + h) & (lane[None, :] == ONES_LANE + i))
            for i in range(GATE_PARTS))
        for h in range(N_HEADS)]).astype(BF16)
    cols = HEADS_PER_PAIR * TQ
    return _attention_call(
        _fox_kernel, "fox_attention", qkv, [gate, perm],
        [pl.BlockSpec((1, s, LANES), lambda b, p: (b, 0, 0)),
         pl.BlockSpec((N_HEADS, LANES, LANES), lambda b, p: (0, 0, 0))],
        [pltpu.VMEM((HEADS_PER_PAIR, s // VT_BLOCK, LANES, VT_BLOCK), BF16),
         pltpu.VMEM((n_tiles, cols, 2 * LANES), BF16),
         pltpu.VMEM((n_tiles, 1, cols), F32),
         pltpu.VMEM((n_tiles, LANES, cols), F32),
         pltpu.VMEM((2, TK, cols), F32),
         pltpu.VMEM((2, 1, cols), F32),
         pltpu.VMEM((2, 1, cols), F32)])


def _sb_attention(qkv):
    b, s, _ = qkv.shape
    n_tiles = s // TQ
    idx = jnp.arange(SB_BLOCK)
    later = (idx[None, :] > idx[:, None]).astype(BF16)
    tt = jnp.concatenate([later, jnp.ones((SUBLANES, SB_BLOCK), BF16)], axis=0)
    cols = HEADS_PER_PAIR * TQ
    return _attention_call(
        _sb_kernel, "sb_attention", qkv, [tt],
        [pl.BlockSpec((SB_BLOCK + SUBLANES, SB_BLOCK), lambda b, p: (0, 0))],
        [pltpu.VMEM((s // VT_BLOCK, LANES, VT_BLOCK), BF16),
         pltpu.VMEM((n_tiles, cols, LANES), BF16),
         pltpu.VMEM((n_tiles, 1, cols), F32),
         pltpu.VMEM((n_tiles, LANES, cols), F32),
         pltpu.VMEM((2, TK, cols), F32),
         pltpu.VMEM((2, TK, cols), BF16)])


def _mm_resid_norm_kernel(a_ref, w_ref, h_ref, g_ref, hs_ref, xn_ref):
    h = _dot(a_ref[...], w_ref[...]) + h_ref[...]
    hs_ref[...] = h
    xn_ref[...] = _rms(h, g_ref[...]).astype(xn_ref.dtype)


def _mm_resid_final_kernel(a_ref, w_ref, h_ref, g_ref, o_ref):
    h = _dot(a_ref[...], w_ref[...]) + h_ref[...]
    o_ref[...] = _rms(h, g_ref[...]).astype(o_ref.dtype)


def _mm_resid_norm(a, w, hs, g, final=False):
    m, k = a.shape
    d = w.shape[1]
    tile = RESID_ROW_TILE
    row = lambda i: (i, 0)
    in_specs = [pl.BlockSpec((tile, k), row),
                pl.BlockSpec((k, d), lambda i: (0, 0)),
                pl.BlockSpec((tile, d), row),
                pl.BlockSpec((1, d), lambda i: (0, 0))]
    if final:
        body = _mm_resid_final_kernel
        out_specs = pl.BlockSpec((tile, d), row)
        out_shape = jax.ShapeDtypeStruct((m, d), F32)
    else:
        body = _mm_resid_norm_kernel
        out_specs = [pl.BlockSpec((tile, d), row), pl.BlockSpec((tile, d), row)]
        out_shape = [jax.ShapeDtypeStruct((m, d), F32), jax.ShapeDtypeStruct((m, d), BF16)]
    return pl.pallas_call(
        body,
        grid=(m // tile,),
        in_specs=in_specs,
        out_specs=out_specs,
        out_shape=out_shape,
        compiler_params=_params(("parallel",), 56),
        name="matmul_residual_norm",
    )(a, w, hs, g.reshape(1, d))


def _ffn_up_kernel(x_ref, w_ref, wc_ref, bc_ref, o_ref, carry_ref, *, tiles_per_seq):
    i = pl.program_id(0)

    @pl.when(i % tiles_per_seq == 0)
    def _():
        carry_ref[...] = jnp.zeros(carry_ref.shape, F32)

    row = lax.broadcasted_iota(jnp.int32, (SUBLANES, FF_CHUNK), 0)

    def shifted(h, prev, shift):
        rolled = pltpu.roll(h, shift, 0)
        head = rolled[:SUBLANES]
        for i in range(shift):
            head = jnp.where(row == i, prev[SUBLANES - shift + i:SUBLANES - shift + i + 1], head)
        return jnp.concatenate([head, rolled[SUBLANES:]], axis=0)

    def conv(h, cols, c, half):
        prev = carry_ref[half, c]
        out = (bc_ref[:, cols] + wc_ref[0:1, cols] * shifted(h, prev, 2)
               + wc_ref[1:2, cols] * shifted(h, prev, 1) + wc_ref[2:3, cols] * h)
        carry_ref[half, c] = h[FF_ROWS - SUBLANES:FF_ROWS]
        return out

    for r in range(x_ref.shape[0] // FF_ROWS):
        rows = slice(r * FF_ROWS, (r + 1) * FF_ROWS)
        x = x_ref[rows, :]
        for c in range(N_FF_CHUNKS):
            ucols = slice(c * FF_CHUNK, (c + 1) * FF_CHUNK)
            gcols = slice(D_FF + c * FF_CHUNK, D_FF + (c + 1) * FF_CHUNK)
            u = conv(_dot(x, w_ref[:, ucols]), ucols, c, 0)
            g = conv(_dot(x, w_ref[:, gcols]), gcols, c, 1)
            act = g * u / (1.0 + jnp.exp(-g))
            o_ref[rows, ucols] = act.astype(o_ref.dtype)


def _ffn_up(xn, w, w_conv, b_conv, seq_len):
    m, k = xn.shape
    tile = RESID_ROW_TILE
    kern = functools.partial(_ffn_up_kernel, tiles_per_seq=seq_len // tile)
    return pl.pallas_call(
        kern,
        grid=(m // tile,),
        in_specs=[pl.BlockSpec((tile, k), lambda i: (i, 0)),
                  pl.BlockSpec((k, 2 * D_FF), lambda i: (0, 0)),
                  pl.BlockSpec((CONV_WIDTH, 2 * D_FF), lambda i: (0, 0)),
                  pl.BlockSpec((1, 2 * D_FF), lambda i: (0, 0))],
        out_specs=pl.BlockSpec((tile, D_FF), lambda i: (i, 0)),
        out_shape=jax.ShapeDtypeStruct((m, D_FF), BF16),
        scratch_shapes=[pltpu.VMEM((2, N_FF_CHUNKS, SUBLANES, FF_CHUNK), F32)],
        compiler_params=_params(("arbitrary",), 56),
        name="ffn_up_conv_gate",
    )(xn, w, w_conv, b_conv)


def kernel(x, attn_norm, ffn_norm, final_norm, fox_w_qkvf, fox_b_f, fox_w_o,
           sb_w_qkv, sb_w_o, ffn_w_up, ffn_w_conv, ffn_b_conv, ffn_w_down):
    b, s, d = x.shape
    m = b * s
    assert d == D_MODEL and s % TQ == 0 and s % ROW_TILE == 0 and s % VT_GROUP == 0 and TQ == TK

    colscale = jnp.concatenate([jnp.full((1, d), ATTN_SCALE * LOG2E, F32),
                                jnp.ones((1, 2 * d), F32)], axis=1)

    hs = x.reshape(m, d)
    xn = _rmsnorm(hs, attn_norm[0], BF16)
    out = None
    for i in range(DEPTH):
        j = i // 2
        if i % 2 == 0:
            w = fox_w_qkvf[j]
            w_gate = jnp.pad(w[:, 3 * d:], ((0, 0), (0, LANES - N_HEADS))).astype(BF16)
            qkv, f_logit = _proj(xn, w[:, :3 * d].astype(BF16), colscale, w_gate)
            b_gate = jnp.pad(fox_b_f[j], (0, LANES - N_HEADS)).reshape(1, LANES)
            gate = _gate_operand(f_logit.reshape(b, s, LANES), b_gate)
            o = _fox_attention(qkv.reshape(b, s, 3 * d), gate)
            w_o = fox_w_o[j]
        else:
            qkv = _proj(xn, sb_w_qkv[j].astype(BF16), colscale)[0]
            o = _sb_attention(qkv.reshape(b, s, 3 * d))
            w_o = sb_w_o[j]
        hs, xn = _mm_resid_norm(o.reshape(m, d), w_o.astype(BF16), hs, ffn_norm[i])
        act = _ffn_up(xn, ffn_w_up[i].astype(BF16), ffn_w_conv[i], ffn_b_conv[i].reshape(1, -1), s)
        w_down = ffn_w_down[i].astype(BF16)
        if i + 1 < DEPTH:
            hs, xn = _mm_resid_norm(act, w_down, hs, attn_norm[i + 1])
        else:
            out = _mm_resid_norm(act, w_down, hs, final_norm, final=True)
    return out.reshape(b, s, d)
```
